```python
import math
import jax, jax.numpy as jnp
from jax import lax
import numpy as np

D_MODEL = 1024
BATCH = 8
SEQ = 4096
DEPTH = 1
DEC_BATCH = 8
DEC_SEQ = 8192
PAST_LEN = 128

CHUNK = 128
SGU_WIDTH = D_MODEL
SGU_GROUPS = 8
SGU_GROUP_DIM = SGU_WIDTH // SGU_GROUPS
N_HEADS = 8
N_KV_HEADS = 2
HEAD_DIM = D_MODEL // N_HEADS
Q_PER_KV = N_HEADS // N_KV_HEADS
WINDOW = 128
BLOCK = 128
N_BUCKETS = 32
MAX_DISTANCE = 128
D_FF = 4 * D_MODEL
EPS = 1e-6
NEG_INF = -1e30
SPLITS = [SGU_WIDTH,
          2 * SGU_WIDTH,
          2 * SGU_WIDTH + N_HEADS * HEAD_DIM,
          2 * SGU_WIDTH + (N_HEADS + N_KV_HEADS) * HEAD_DIM,
          2 * SGU_WIDTH + (N_HEADS + 2 * N_KV_HEADS) * HEAD_DIM,
          2 * SGU_WIDTH + (N_HEADS + 2 * N_KV_HEADS) * HEAD_DIM + D_MODEL]
IN_COLS = SPLITS[-1] + D_MODEL

kernel_name = "hybrid_sgu_window_gqa_encoder"


def rms_norm(x, g):
    xf = x.astype(jnp.float32)
    y = xf * lax.rsqrt(jnp.mean(xf * xf, axis=-1, keepdims=True) + EPS)
    return (y * g.astype(jnp.float32)).astype(x.dtype)


def t5_bucket(rel):
    nb = N_BUCKETS // 2
    max_exact = nb // 2
    ret = jnp.where(rel > 0, nb, 0)
    n = jnp.abs(rel)
    nf = jnp.maximum(n, 1).astype(jnp.float32)
    large = max_exact + (jnp.log(nf / max_exact) / math.log(MAX_DISTANCE / max_exact)
                         * (nb - max_exact)).astype(jnp.int32)
    large = jnp.minimum(large, nb - 1)
    return ret + jnp.where(n < max_exact, n, large)


def sgu_mixer(u, v, g_norm, w_s, b_s):
    B, S, _ = u.shape
    u = jax.nn.gelu(u)
    v = rms_norm(jax.nn.gelu(v), g_norm)
    vc = v.reshape(B, S // CHUNK, CHUNK, SGU_GROUPS, SGU_GROUP_DIM)
    mixed = jnp.einsum('gpq,bnqgc->bnpgc', w_s, vc) + b_s.T[None, None, :, :, None]
    return u * mixed.reshape(B, S, SGU_WIDTH)


def band_blocks(t):
    B, S, K, D = t.shape
    nblk = S // BLOCK
    tp = jnp.pad(t, ((0, 0), (BLOCK, BLOCK), (0, 0), (0, 0))).reshape(B, nblk + 2, BLOCK, K, D)
    return jnp.concatenate([tp[:, :-2], tp[:, 1:-1], tp[:, 2:]], axis=2)


def window_attention(q, k, v, q_g, k_g, sink, rel_bias):
    B, S, _ = q.shape
    nblk = S // BLOCK
    q = rms_norm(q.reshape(B, S, N_HEADS, HEAD_DIM), q_g)
    k = rms_norm(k.reshape(B, S, N_KV_HEADS, HEAD_DIM), k_g)
    v = v.reshape(B, S, N_KV_HEADS, HEAD_DIM)
    qb = q.reshape(B, nblk, BLOCK, N_KV_HEADS, Q_PER_KV, HEAD_DIM)
    kb = band_blocks(k)
    vb = band_blocks(v)
    s = jnp.einsum('bnpkgd,bnrkd->bnkgpr', qb, kb).astype(jnp.float32) * (HEAD_DIM ** -0.5)
    p_idx = jnp.arange(BLOCK, dtype=jnp.int32)[:, None]
    r_idx = jnp.arange(3 * BLOCK, dtype=jnp.int32)[None, :]
    rel = r_idx - BLOCK - p_idx
    bias = rel_bias.astype(jnp.float32)[t5_bucket(rel)]
    bias = jnp.transpose(bias, (2, 0, 1)).reshape(N_KV_HEADS, Q_PER_KV, BLOCK, 3 * BLOCK)
    in_window = jnp.abs(rel) <= WINDOW
    key_pos = (jnp.arange(nblk, dtype=jnp.int32)[:, None] - 1) * BLOCK + r_idx
    in_range = (key_pos >= 0) & (key_pos < S)
    mask = in_window[None, :, :] & in_range[:, None, :]
    s = jnp.where(mask[None, :, None, None, :, :], s + bias[None, None], NEG_INF)
    sink_l = sink.astype(jnp.float32).reshape(N_KV_HEADS, Q_PER_KV)[None, None, :, :, None, None]
    m = jnp.maximum(jnp.max(s, axis=-1, keepdims=True), sink_l)
    e = jnp.exp(s - m)
    probs = e / (jnp.sum(e, axis=-1, keepdims=True) + jnp.exp(sink_l - m))
    o = jnp.einsum('bnkgpr,bnrkd->bnpkgd', probs.astype(vb.dtype), vb)
    return o.reshape(B, S, N_HEADS * HEAD_DIM)


def encoder_layer(x, rel_bias, norm1_g, w_in, sgu_g, w_s, b_s, q_g, k_g, sink, w_o,
                  norm2_g, w_ff1, w_ff2):
    h = rms_norm(x, norm1_g)
    z = h @ w_in
    u, v, q, k, vv, ga, gb = jnp.split(z, SPLITS, axis=-1)
    ya = sgu_mixer(u, v, sgu_g, w_s, b_s)
    yb = window_attention(q, k, vv, q_g, k_g, sink, rel_bias)
    merged = jax.nn.sigmoid(ga) * ya + jax.nn.sigmoid(gb) * yb
    x = x + merged @ w_o
    h2 = rms_norm(x, norm2_g)
    return x + jnp.square(jax.nn.relu(h2 @ w_ff1)) @ w_ff2


def setup_inputs(seed: int = 0) -> dict:
    key = jax.random.key(seed)
    ks = jax.random.split(key, 16)
    f32 = jnp.float32
    nrm = lambda k, shape, s: jax.random.normal(k, shape, f32) * s
    return {
        "x_prompt": nrm(ks[0], (BATCH, SEQ, D_MODEL), 1.0),
        "x_sample": nrm(ks[1], (DEC_BATCH, DEC_SEQ, D_MODEL), 1.0),
        "rel_bias": nrm(ks[2], (N_BUCKETS, N_HEADS), 0.5),
        "norm1_g": 1.0 + nrm(ks[3], (DEPTH, D_MODEL), 0.02),
        "w_in": nrm(ks[4], (DEPTH, D_MODEL, IN_COLS), D_MODEL ** -0.5),
        "sgu_norm_g": 1.0 + nrm(ks[5], (DEPTH, SGU_WIDTH), 0.02),
        "w_spatial": nrm(ks[6], (DEPTH, SGU_GROUPS, CHUNK, CHUNK), CHUNK ** -0.5),
        "b_spatial": 1.0 + nrm(ks[7], (DEPTH, SGU_GROUPS, CHUNK), 0.02),
        "q_norm_g": 1.0 + nrm(ks[8], (DEPTH, HEAD_DIM), 0.02),
        "k_norm_g": 1.0 + nrm(ks[9], (DEPTH, HEAD_DIM), 0.02),
        "sink_logit": nrm(ks[10], (DEPTH, N_HEADS), 0.5),
        "w_o": nrm(ks[11], (DEPTH, D_MODEL, D_MODEL), D_MODEL ** -0.5),
        "norm2_g": 1.0 + nrm(ks[12], (DEPTH, D_MODEL), 0.02),
        "w_ff1": nrm(ks[13], (DEPTH, D_MODEL, D_FF), D_MODEL ** -0.5),
        "w_ff2": nrm(ks[14], (DEPTH, D_FF, D_MODEL), D_FF ** -0.5),
    }


def reference(x_prompt, x_sample, rel_bias, norm1_g, w_in, sgu_norm_g, w_spatial, b_spatial,
              q_norm_g, k_norm_g, sink_logit, w_o, norm2_g, w_ff1, w_ff2):
    def trunk(x):
        for l in range(DEPTH):
            x = encoder_layer(x, rel_bias, norm1_g[l], w_in[l], sgu_norm_g[l], w_spatial[l],
                              b_spatial[l], q_norm_g[l], k_norm_g[l], sink_logit[l], w_o[l],
                              norm2_g[l], w_ff1[l], w_ff2[l])
        return x
    y_prompt = trunk(x_prompt)
    y_sample = trunk(x_sample)
    return (y_prompt, y_sample)
```

```python
import functools
import math

import jax
import jax.numpy as jnp
from jax import lax
from jax.experimental import pallas as pl
from jax.experimental.pallas import tpu as pltpu

D_MODEL = 1024
CHUNK = 128
SGU_GROUPS = 8
N_HEADS = 8
N_KV_HEADS = 2
HEAD_DIM = 128
Q_PER_KV = N_HEADS // N_KV_HEADS
WINDOW = 128
BLOCK = 128
BAND = 3 * BLOCK
N_BUCKETS = 32
MAX_DISTANCE = 128
D_FF = 4 * D_MODEL
EPS = 1e-6
NEG_INF = -1e30
KV_COLS = 2 * N_KV_HEADS * HEAD_DIM
IN_COLS = 2 * D_MODEL + D_MODEL + KV_COLS + 2 * D_MODEL

F32 = jnp.float32
BF16 = jnp.bfloat16

VMEM_LIMIT_BYTES = 56 * 1024 * 1024


def _t5_bucket(rel):
    nb = N_BUCKETS // 2
    max_exact = nb // 2
    ret = jnp.where(rel > 0, nb, 0)
    n = jnp.abs(rel)
    nf = jnp.maximum(n, 1).astype(F32)
    large = max_exact + (jnp.log(nf / max_exact) / math.log(MAX_DISTANCE / max_exact)
                         * (nb - max_exact)).astype(jnp.int32)
    large = jnp.minimum(large, nb - 1)
    return ret + jnp.where(n < max_exact, n, large)


def _rms(x, g):
    ms = jnp.mean(x * x, axis=-1, keepdims=True)
    return x * lax.rsqrt(ms + EPS) * g


def _gelu(x):
    c = math.sqrt(2.0 / math.pi)
    cdf = 0.5 * (1.0 + jnp.tanh(c * (x + 0.044715 * (x * x * x))))
    return x * cdf


def _sigmoid(x):
    return 1.0 / (1.0 + jnp.exp(-x))


def _bias_table_body(bucket_ref, relb_ref, out_ref):
    bucket = bucket_ref[...]
    p = lax.broadcasted_iota(jnp.int32, (BLOCK, BAND), 0)
    r = lax.broadcasted_iota(jnp.int32, (BLOCK, BAND), 1)
    in_win = jnp.abs(r - BLOCK - p) <= WINDOW
    for h in range(N_HEADS):
        acc = jnp.zeros((BLOCK, BAND), F32)
        for b in range(N_BUCKETS):
            acc = jnp.where(bucket == b, relb_ref[b, h], acc)
        base = jnp.where(in_win, acc, NEG_INF)
        out_ref[0, h] = jnp.where(r >= BLOCK, base, NEG_INF)
        out_ref[1, h] = base
        out_ref[2, h] = jnp.where(r < 2 * BLOCK, base, NEG_INF)


def _bias_table(rel_bias):
    p_idx = jnp.arange(BLOCK, dtype=jnp.int32)[:, None]
    r_idx = jnp.arange(BAND, dtype=jnp.int32)[None, :]
    bucket = _t5_bucket(r_idx - BLOCK - p_idx).astype(jnp.int32)
    return pl.pallas_call(
        _bias_table_body,
        out_shape=jax.ShapeDtypeStruct((3, N_HEADS, BLOCK, BAND), F32),
        in_specs=[pl.BlockSpec(memory_space=pltpu.VMEM),
                  pl.BlockSpec(memory_space=pltpu.SMEM)],
        out_specs=pl.BlockSpec(memory_space=pltpu.VMEM),
        name="bias_table",
    )(bucket, rel_bias.astype(F32))


def _inproj_body(x_ref, g1_ref, w_ref, sgug_ref, qg_ref, kg_ref,
                 ua_ref, vn_ref, q_ref, kv_ref, ga_ref, gb_ref):
    h = _rms(x_ref[...], g1_ref[...]).astype(BF16)

    def proj(lo, width):
        return jnp.dot(h, w_ref[:, lo:lo + width], preferred_element_type=F32)

    ua_ref[...] = _gelu(proj(0, D_MODEL)).astype(BF16)
    vn_ref[...] = _rms(_gelu(proj(D_MODEL, D_MODEL)), sgug_ref[...]).astype(BF16)

    zq = proj(2 * D_MODEL, D_MODEL)
    qg = qg_ref[...] * (HEAD_DIM ** -0.5)
    for hd in range(N_HEADS):
        sl = slice(hd * HEAD_DIM, (hd + 1) * HEAD_DIM)
        q_ref[:, sl] = _rms(zq[:, sl], qg).astype(BF16)

    zkv = proj(3 * D_MODEL, KV_COLS)
    for hd in range(N_KV_HEADS):
        sl = slice(hd * HEAD_DIM, (hd + 1) * HEAD_DIM)
        kv_ref[:, sl] = _rms(zkv[:, sl], kg_ref[...]).astype(BF16)
    vsl = slice(N_KV_HEADS * HEAD_DIM, KV_COLS)
    kv_ref[:, vsl] = zkv[:, vsl].astype(BF16)

    ga_ref[...] = _sigmoid(proj(3 * D_MODEL + KV_COLS, D_MODEL)).astype(BF16)
    gb_ref[...] = _sigmoid(proj(4 * D_MODEL + KV_COLS, D_MODEL)).astype(BF16)


def _inproj(x2d, g1, w_in, sgu_g, q_g, k_g, tile):
    n = x2d.shape[0]
    row = lambda width: pl.BlockSpec((tile, width), lambda i: (i, 0))
    const = lambda shape: pl.BlockSpec(shape, lambda i: (0,) * len(shape), pipeline_mode=pl.Buffered(1))
    out = lambda width: jax.ShapeDtypeStruct((n, width), BF16)
    return pl.pallas_call(
        _inproj_body,
        grid=(n // tile,),
        in_specs=[row(D_MODEL), const((1, D_MODEL)), const((D_MODEL, IN_COLS)),
                  const((1, D_MODEL)), const((1, HEAD_DIM)), const((1, HEAD_DIM))],
        out_specs=[row(D_MODEL), row(D_MODEL), row(D_MODEL), row(KV_COLS), row(D_MODEL), row(D_MODEL)],
        out_shape=[out(D_MODEL), out(D_MODEL), out(D_MODEL), out(KV_COLS), out(D_MODEL), out(D_MODEL)],
        compiler_params=pltpu.CompilerParams(dimension_semantics=("arbitrary",),
                                             vmem_limit_bytes=VMEM_LIMIT_BYTES),
        name="inproj",
    )(x2d, g1, w_in, sgu_g, q_g, k_g)


def _mixer_body(tiles_per_seq, nblk,
                x_ref, ua_ref, vn_ref, q_ref, kv_ref, kvp_ref, kvn_ref, ga_ref, gb_ref,
                tbl_ref, sink_ref, ws_ref, bs_ref, wo_ref, n2g_ref, w1_ref, w2_ref,
                o_ref, band_ref, merged_ref):
    tile = x_ref.shape[0]
    jt = pl.program_id(0) % tiles_per_seq

    band_ref[0:BLOCK, :] = kvp_ref[...]
    band_ref[BLOCK:BLOCK + tile, :] = kv_ref[...]
    band_ref[BLOCK + tile:, :] = kvn_ref[...]

    def block(i, carry):
        r0 = pl.multiple_of(i * BLOCK, BLOCK)
        rows = pl.ds(r0, BLOCK)
        first = jnp.logical_and(jt == 0, i == 0)
        last = jnp.logical_and(jt == tiles_per_seq - 1, i == nblk - 1)
        variant = jnp.where(first, 0, jnp.where(last, 2, 1))
        kv = band_ref[pl.ds(r0, BAND), :]
        for kh in range(N_KV_HEADS):
            kmat = kv[:, kh * HEAD_DIM:(kh + 1) * HEAD_DIM]
            vmat = kv[:, (N_KV_HEADS + kh) * HEAD_DIM:(N_KV_HEADS + kh + 1) * HEAD_DIM]
            heads = [kh * Q_PER_KV + g for g in range(Q_PER_KV)]
            qcat = jnp.concatenate(
                [q_ref[rows, hd * HEAD_DIM:(hd + 1) * HEAD_DIM] for hd in heads], axis=0)
            s = lax.dot_general(qcat, kmat, (((1,), (1,)), ((), ())), preferred_element_type=F32)
            probs, denoms = [], []
            for g, hd in enumerate(heads):
                sg = s[g * BLOCK:(g + 1) * BLOCK, :] + tbl_ref[variant, hd]
                sink = sink_ref[hd]
                m = jnp.maximum(jnp.max(sg, axis=-1, keepdims=True), sink)
                e = jnp.exp(sg - m)
                denoms.append(jnp.sum(e, axis=-1, keepdims=True) + jnp.exp(sink - m))
                probs.append(e.astype(BF16))
            o = jnp.dot(jnp.concatenate(probs, axis=0), vmat, preferred_element_type=F32)
            for g, hd in enumerate(heads):
                cols = slice(hd * HEAD_DIM, (hd + 1) * HEAD_DIM)
                yb = o[g * BLOCK:(g + 1) * BLOCK, :] / denoms[g]
                mixed = jnp.dot(ws_ref[hd], vn_ref[rows, cols], preferred_element_type=F32) + bs_ref[hd]
                ya = ua_ref[rows, cols].astype(F32) * mixed
                merged = ga_ref[rows, cols].astype(F32) * ya + gb_ref[rows, cols].astype(F32) * yb
                merged_ref[rows, cols] = merged.astype(BF16)
        return carry

    lax.fori_loop(0, nblk, block, 0)

    x1 = x_ref[...] + jnp.dot(merged_ref[...], wo_ref[...], preferred_element_type=F32)
    h2 = _rms(x1, n2g_ref[...]).astype(BF16)
    acc = x1
    for c in range(D_FF // D_MODEL):
        sl = slice(c * D_MODEL, (c + 1) * D_MODEL)
        hid = jnp.dot(h2, w1_ref[:, sl], preferred_element_type=F32)
        hid = jnp.square(jnp.maximum(hid, 0.0)).astype(BF16)
        acc = acc + jnp.dot(hid, w2_ref[sl, :], preferred_element_type=F32)
    o_ref[...] = acc


def _mixer(x2d, acts, tbl, sink, ws, bs, wo, n2g, w1, w2, seq, tile):
    ua, vn, q, kv, ga, gb = acts
    n = x2d.shape[0]
    nblk = tile // BLOCK
    tiles_per_seq = seq // tile
    nblocks_total = n // BLOCK
    row = lambda width: pl.BlockSpec((tile, width), lambda j: (j, 0))
    const = lambda shape: pl.BlockSpec(shape, lambda j: (0,) * len(shape), pipeline_mode=pl.Buffered(1))
    prev = pl.BlockSpec((BLOCK, KV_COLS), lambda j: (jnp.maximum(j * nblk - 1, 0), 0))
    nxt = pl.BlockSpec((BLOCK, KV_COLS), lambda j: (jnp.minimum((j + 1) * nblk, nblocks_total - 1), 0))
    return pl.pallas_call(
        functools.partial(_mixer_body, tiles_per_seq, nblk),
        grid=(n // tile,),
        in_specs=[row(D_MODEL), row(D_MODEL), row(D_MODEL), row(D_MODEL), row(KV_COLS), prev, nxt,
                  row(D_MODEL), row(D_MODEL),
                  const((3, N_HEADS, BLOCK, BAND)),
                  pl.BlockSpec(memory_space=pltpu.SMEM),
                  const((SGU_GROUPS, CHUNK, CHUNK)), const((SGU_GROUPS, CHUNK, CHUNK)),
                  const((D_MODEL, D_MODEL)), const((1, D_MODEL)),
                  const((D_MODEL, D_FF)), const((D_FF, D_MODEL))],
        out_specs=row(D_MODEL),
        out_shape=jax.ShapeDtypeStruct((n, D_MODEL), F32),
        scratch_shapes=[pltpu.VMEM((tile + 2 * BLOCK, KV_COLS), BF16),
                        pltpu.VMEM((tile, D_MODEL), BF16)],
        compiler_params=pltpu.CompilerParams(dimension_semantics=("arbitrary",),
                                             vmem_limit_bytes=VMEM_LIMIT_BYTES),
        name="mixer",
    )(x2d, ua, vn, q, kv, kv, kv, ga, gb, tbl, sink, ws, bs, wo, n2g, w1, w2)


def kernel(x_prompt, x_sample, rel_bias, norm1_g, w_in, sgu_norm_g, w_spatial, b_spatial,
           q_norm_g, k_norm_g, sink_logit, w_o, norm2_g, w_ff1, w_ff2):
    assert norm1_g.shape[0] == 1, "single-layer trunk"
    tbl = _bias_table(rel_bias)
    g1 = norm1_g[0].reshape(1, D_MODEL)
    sgu_g = sgu_norm_g[0].reshape(1, D_MODEL)
    q_g = q_norm_g[0].reshape(1, HEAD_DIM)
    k_g = k_norm_g[0].reshape(1, HEAD_DIM)
    n2g = norm2_g[0].reshape(1, D_MODEL)
    w_in_b = w_in[0].astype(BF16)
    ws = w_spatial[0].astype(BF16)
    bs = jnp.broadcast_to(b_spatial[0][:, :, None], (SGU_GROUPS, CHUNK, CHUNK)).astype(F32)
    wo = w_o[0].astype(BF16)
    w1 = w_ff1[0].astype(BF16)
    w2 = w_ff2[0].astype(BF16)
    sink = sink_logit[0].astype(F32)

    def trunk(x):
        b, s, d = x.shape
        x2d = x.reshape(b * s, d)
        acts = _inproj(x2d, g1, w_in_b, sgu_g, q_g, k_g, tile=256)
        y = _mixer(x2d, acts, tbl, sink, ws, bs, wo, n2g, w1, w2, seq=s, tile=256)
        return y.reshape(b, s, d)

    return (trunk(x_prompt), trunk(x_sample))
```

```python
import functools
import math

import jax
import jax.numpy as jnp
from jax import lax
from jax.experimental import pallas as pl
from jax.experimental.pallas import tpu as pltpu

D_MODEL = 1024
CHUNK = 128
SGU_GROUPS = 8
N_HEADS = 8
N_KV_HEADS = 2
HEAD_DIM = 128
Q_PER_KV = N_HEADS // N_KV_HEADS
WINDOW = 128
BLOCK = 128
BAND = 3 * BLOCK
N_BUCKETS = 32
MAX_DISTANCE = 128
D_FF = 4 * D_MODEL
EPS = 1e-6
NEG_INF = -1e30
KV_COLS = 2 * N_KV_HEADS * HEAD_DIM
IN_COLS = 2 * D_MODEL + D_MODEL + KV_COLS + 2 * D_MODEL

F32 = jnp.float32
BF16 = jnp.bfloat16

VMEM_LIMIT_BYTES = 56 * 1024 * 1024


def _t5_bucket(rel):
    nb = N_BUCKETS // 2
    max_exact = nb // 2
    ret = jnp.where(rel > 0, nb, 0)
    n = jnp.abs(rel)
    nf = jnp.maximum(n, 1).astype(F32)
    large = max_exact + (jnp.log(nf / max_exact) / math.log(MAX_DISTANCE / max_exact)
                         * (nb - max_exact)).astype(jnp.int32)
    large = jnp.minimum(large, nb - 1)
    return ret + jnp.where(n < max_exact, n, large)


def _rms(x, g):
    ms = jnp.mean(x * x, axis=-1, keepdims=True)
    return x * lax.rsqrt(ms + EPS) * g


def _gelu(x):
    c = math.sqrt(2.0 / math.pi)
    cdf = 0.5 * (1.0 + jnp.tanh(c * (x + 0.044715 * (x * x * x))))
    return x * cdf


def _sigmoid(x):
    return 1.0 / (1.0 + jnp.exp(-x))


def _bias_table_body(bucket_ref, relb_ref, out_ref):
    bucket = bucket_ref[...]
    p = lax.broadcasted_iota(jnp.int32, (BLOCK, BAND), 0)
    r = lax.broadcasted_iota(jnp.int32, (BLOCK, BAND), 1)
    in_win = jnp.abs(r - BLOCK - p) <= WINDOW
    for h in range(N_HEADS):
        acc = jnp.zeros((BLOCK, BAND), F32)
        for b in range(N_BUCKETS):
            acc = jnp.where(bucket == b, relb_ref[b, h], acc)
        base = jnp.where(in_win, acc, NEG_INF)
        out_ref[0, h] = jnp.where(r >= BLOCK, base, NEG_INF)
        out_ref[1, h] = base
        out_ref[2, h] = jnp.where(r < 2 * BLOCK, base, NEG_INF)


def _bias_table(rel_bias):
    p_idx = jnp.arange(BLOCK, dtype=jnp.int32)[:, None]
    r_idx = jnp.arange(BAND, dtype=jnp.int32)[None, :]
    bucket = _t5_bucket(r_idx - BLOCK - p_idx).astype(jnp.int32)
    return pl.pallas_call(
        _bias_table_body,
        out_shape=jax.ShapeDtypeStruct((3, N_HEADS, BLOCK, BAND), F32),
        in_specs=[pl.BlockSpec(memory_space=pltpu.VMEM),
                  pl.BlockSpec(memory_space=pltpu.SMEM)],
        out_specs=pl.BlockSpec(memory_space=pltpu.VMEM),
        name="bias_table",
    )(bucket, rel_bias)


def _inproj_body(x_ref, g1_ref, w_ref, sgug_ref, qg_ref, kg_ref,
                 ua_ref, vn_ref, q_ref, kv_ref, ga_ref, gb_ref):
    h = _rms(x_ref[...], g1_ref[...]).astype(BF16)

    def proj(lo, width):
        return jnp.dot(h, w_ref[:, lo:lo + width], preferred_element_type=F32)

    ua_ref[...] = _gelu(proj(0, D_MODEL)).astype(BF16)
    vn_ref[...] = _rms(_gelu(proj(D_MODEL, D_MODEL)), sgug_ref[...]).astype(BF16)

    zq = proj(2 * D_MODEL, D_MODEL)
    qg = qg_ref[...] * (HEAD_DIM ** -0.5)
    for hd in range(N_HEADS):
        sl = slice(hd * HEAD_DIM, (hd + 1) * HEAD_DIM)
        q_ref[:, sl] = _rms(zq[:, sl], qg).astype(BF16)

    zkv = proj(3 * D_MODEL, KV_COLS)
    for hd in range(N_KV_HEADS):
        sl = slice(hd * HEAD_DIM, (hd + 1) * HEAD_DIM)
        kv_ref[:, sl] = _rms(zkv[:, sl], kg_ref[...]).astype(BF16)
    vsl = slice(N_KV_HEADS * HEAD_DIM, KV_COLS)
    kv_ref[:, vsl] = zkv[:, vsl].astype(BF16)

    ga_ref[...] = _sigmoid(proj(3 * D_MODEL + KV_COLS, D_MODEL)).astype(BF16)
    gb_ref[...] = _sigmoid(proj(4 * D_MODEL + KV_COLS, D_MODEL)).astype(BF16)


def _inproj(x2d, g1, w_in, sgu_g, q_g, k_g, tile):
    n = x2d.shape[0]
    row = lambda width: pl.BlockSpec((tile, width), lambda i: (i, 0))
    const = lambda shape: pl.BlockSpec(shape, lambda i: (0,) * len(shape), pipeline_mode=pl.Buffered(1))
    out = lambda width: jax.ShapeDtypeStruct((n, width), BF16)
    return pl.pallas_call(
        _inproj_body,
        grid=(n // tile,),
        in_specs=[row(D_MODEL), const((1, D_MODEL)), const((D_MODEL, IN_COLS)),
                  const((1, D_MODEL)), const((1, HEAD_DIM)), const((1, HEAD_DIM))],
        out_specs=[row(D_MODEL), row(D_MODEL), row(D_MODEL), row(KV_COLS), row(D_MODEL), row(D_MODEL)],
        out_shape=[out(D_MODEL), out(D_MODEL), out(D_MODEL), out(KV_COLS), out(D_MODEL), out(D_MODEL)],
        compiler_params=pltpu.CompilerParams(dimension_semantics=("arbitrary",),
                                             vmem_limit_bytes=VMEM_LIMIT_BYTES),
        name="inproj",
    )(x2d, g1, w_in, sgu_g, q_g, k_g)


def _mixer_body(tiles_per_seq, nblk,
                x_ref, ua_ref, vn_ref, q_ref, kv_ref, kvp_ref, kvn_ref, ga_ref, gb_ref,
                tbl_ref, sink_ref, ws_ref, bs_ref, wo_ref, n2g_ref, w1_ref, w2_ref,
                o_ref, band_ref, merged_ref):
    tile = x_ref.shape[0]
    jt = pl.program_id(0) % tiles_per_seq

    band_ref[0:BLOCK, :] = kvp_ref[...]
    band_ref[BLOCK:BLOCK + tile, :] = kv_ref[...]
    band_ref[BLOCK + tile:, :] = kvn_ref[...]

    def block(i, carry):
        r0 = pl.multiple_of(i * BLOCK, BLOCK)
        rows = pl.ds(r0, BLOCK)
        first = jnp.logical_and(jt == 0, i == 0)
        last = jnp.logical_and(jt == tiles_per_seq - 1, i == nblk - 1)
        variant = jnp.where(first, 0, jnp.where(last, 2, 1))
        kv = band_ref[pl.ds(r0, BAND), :]
        for kh in range(N_KV_HEADS):
            kmat = kv[:, kh * HEAD_DIM:(kh + 1) * HEAD_DIM]
            vmat = kv[:, (N_KV_HEADS + kh) * HEAD_DIM:(N_KV_HEADS + kh + 1) * HEAD_DIM]
            heads = [kh * Q_PER_KV + g for g in range(Q_PER_KV)]
            qcat = jnp.concatenate(
                [q_ref[rows, hd * HEAD_DIM:(hd + 1) * HEAD_DIM] for hd in heads], axis=0)
            s = lax.dot_general(qcat, kmat, (((1,), (1,)), ((), ())), preferred_element_type=F32)
            probs, denoms = [], []
            for g, hd in enumerate(heads):
                sg = s[g * BLOCK:(g + 1) * BLOCK, :] + tbl_ref[variant, hd]
                sink = sink_ref[hd]
                m = jnp.maximum(jnp.max(sg, axis=-1, keepdims=True), sink)
                e = jnp.exp(sg - m)
                denoms.append(jnp.sum(e, axis=-1, keepdims=True) + jnp.exp(sink - m))
                probs.append(e.astype(BF16))
            o = jnp.dot(jnp.concatenate(probs, axis=0), vmat, preferred_element_type=F32)
            for g, hd in enumerate(heads):
                cols = slice(hd * HEAD_DIM, (hd + 1) * HEAD_DIM)
                yb = o[g * BLOCK:(g + 1) * BLOCK, :] / denoms[g]
                mixed = jnp.dot(ws_ref[hd], vn_ref[rows, cols], preferred_element_type=F32) + bs_ref[hd]
                ya = ua_ref[rows, cols].astype(F32) * mixed
                merged = ga_ref[rows, cols].astype(F32) * ya + gb_ref[rows, cols].astype(F32) * yb
                merged_ref[rows, cols] = merged.astype(BF16)
        return carry

    lax.fori_loop(0, nblk, block, 0)

    x1 = x_ref[...] + jnp.dot(merged_ref[...], wo_ref[...], preferred_element_type=F32)
    h2 = _rms(x1, n2g_ref[...]).astype(BF16)
    acc = x1
    for c in range(D_FF // D_MODEL):
        sl = slice(c * D_MODEL, (c + 1) * D_MODEL)
        hid = jnp.dot(h2, w1_ref[:, sl], preferred_element_type=F32)
        hid = jnp.square(jnp.maximum(hid, 0.0)).astype(BF16)
        acc = acc + jnp.dot(hid, w2_ref[sl, :], preferred_element_type=F32)
    o_ref[...] = acc


def _mixer(x2d, acts, tbl, sink, ws, bs, wo, n2g, w1, w2, seq, tile):
    ua, vn, q, kv, ga, gb = acts
    n = x2d.shape[0]
    nblk = tile // BLOCK
    tiles_per_seq = seq // tile
    nblocks_total = n // BLOCK
    row = lambda width: pl.BlockSpec((tile, width), lambda j: (j, 0))
    const = lambda shape: pl.BlockSpec(shape, lambda j: (0,) * len(shape), pipeline_mode=pl.Buffered(1))
    prev = pl.BlockSpec((BLOCK, KV_COLS), lambda j: (jnp.maximum(j * nblk - 1, 0), 0))
    nxt = pl.BlockSpec((BLOCK, KV_COLS), lambda j: (jnp.minimum((j + 1) * nblk, nblocks_total - 1), 0))
    return pl.pallas_call(
        functools.partial(_mixer_body, tiles_per_seq, nblk),
        grid=(n // tile,),
        in_specs=[row(D_MODEL), row(D_MODEL), row(D_MODEL), row(D_MODEL), row(KV_COLS), prev, nxt,
                  row(D_MODEL), row(D_MODEL),
                  const((3, N_HEADS, BLOCK, BAND)),
                  pl.BlockSpec(memory_space=pltpu.SMEM),
                  const((SGU_GROUPS, CHUNK, CHUNK)), const((SGU_GROUPS, CHUNK, CHUNK)),
                  const((D_MODEL, D_MODEL)), const((1, D_MODEL)),
                  const((D_MODEL, D_FF)), const((D_FF, D_MODEL))],
        out_specs=row(D_MODEL),
        out_shape=jax.ShapeDtypeStruct((n, D_MODEL), F32),
        scratch_shapes=[pltpu.VMEM((tile + 2 * BLOCK, KV_COLS), BF16),
                        pltpu.VMEM((tile, D_MODEL), BF16)],
        compiler_params=pltpu.CompilerParams(dimension_semantics=("arbitrary",),
                                             vmem_limit_bytes=VMEM_LIMIT_BYTES),
        name="mixer",
    )(x2d, ua, vn, q, kv, kv, kv, ga, gb, tbl, sink, ws, bs, wo, n2g, w1, w2)


def kernel(x_prompt, x_sample, rel_bias, norm1_g, w_in, sgu_norm_g, w_spatial, b_spatial,
           q_norm_g, k_norm_g, sink_logit, w_o, norm2_g, w_ff1, w_ff2):
    assert norm1_g.shape[0] == 1, "single-layer trunk"
    tbl = _bias_table(rel_bias)
    g1 = norm1_g[0].reshape(1, D_MODEL)
    sgu_g = sgu_norm_g[0].reshape(1, D_MODEL)
    q_g = q_norm_g[0].reshape(1, HEAD_DIM)
    k_g = k_norm_g[0].reshape(1, HEAD_DIM)
    n2g = norm2_g[0].reshape(1, D_MODEL)
    w_in_b = w_in[0].astype(BF16)
    ws = w_spatial[0].astype(BF16)
    bs = jnp.broadcast_to(b_spatial[0][:, :, None], (SGU_GROUPS, CHUNK, CHUNK))
    wo = w_o[0].astype(BF16)
    w1 = w_ff1[0].astype(BF16)
    w2 = w_ff2[0].astype(BF16)
    sink = sink_logit[0]

    def trunk(x):
        b, s, d = x.shape
        x2d = x.reshape(b * s, d)
        acts = _inproj(x2d, g1, w_in_b, sgu_g, q_g, k_g, tile=512)
        y = _mixer(x2d, acts, tbl, sink, ws, bs, wo, n2g, w1, w2, seq=s, tile=512)
        return y.reshape(b, s, d)

    return (trunk(x_prompt), trunk(x_sample))
```

```python
import functools
import math

import jax
import jax.numpy as jnp
from jax import lax
from jax.experimental import pallas as pl
from jax.experimental.pallas import tpu as pltpu

D_MODEL = 1024
CHUNK = 128
SGU_GROUPS = 8
N_HEADS = 8
N_KV_HEADS = 2
HEAD_DIM = 128
Q_PER_KV = N_HEADS // N_KV_HEADS
WINDOW = 128
BLOCK = 128
BAND = 3 * BLOCK
N_BUCKETS = 32
MAX_DISTANCE = 128
D_FF = 4 * D_MODEL
EPS = 1e-6
NEG_INF = -1e30
KV_COLS = 2 * N_KV_HEADS * HEAD_DIM
IN_COLS = 2 * D_MODEL + D_MODEL + KV_COLS + 2 * D_MODEL
PROJ_COLS = 256
PROJ_ROWS = 256
FFN_ROWS = 256

F32 = jnp.float32
BF16 = jnp.bfloat16

VMEM_LIMIT_BYTES = 56 * 1024 * 1024


def _t5_bucket(rel):
    nb = N_BUCKETS // 2
    max_exact = nb // 2
    ret = jnp.where(rel > 0, nb, 0)
    n = jnp.abs(rel)
    nf = jnp.maximum(n, 1).astype(F32)
    large = max_exact + (jnp.log(nf / max_exact) / math.log(MAX_DISTANCE / max_exact)
                         * (nb - max_exact)).astype(jnp.int32)
    large = jnp.minimum(large, nb - 1)
    return ret + jnp.where(n < max_exact, n, large)


def _rms(x, g):
    ms = jnp.mean(x * x, axis=-1, keepdims=True)
    return x * lax.rsqrt(ms + EPS) * g


def _gelu(x):
    c = math.sqrt(2.0 / math.pi)
    hx = 0.5 * x
    return hx + hx * jnp.tanh(x * (c + (c * 0.044715) * (x * x)))


def _sigmoid(x):
    return 1.0 / (1.0 + jnp.exp(-x))


def _bias_table_body(bucket_ref, relb_ref, out_ref):
    bucket = bucket_ref[...]
    p = lax.broadcasted_iota(jnp.int32, (BLOCK, BAND), 0)
    r = lax.broadcasted_iota(jnp.int32, (BLOCK, BAND), 1)
    in_win = jnp.abs(r - BLOCK - p) <= WINDOW
    for h in range(N_HEADS):
        acc = jnp.zeros((BLOCK, BAND), F32)
        for b in range(N_BUCKETS):
            acc = jnp.where(bucket == b, relb_ref[b, h], acc)
        base = jnp.where(in_win, acc, NEG_INF)
        out_ref[0, h] = jnp.where(r >= BLOCK, base, NEG_INF)
        out_ref[1, h] = base
        out_ref[2, h] = jnp.where(r < 2 * BLOCK, base, NEG_INF)


def _bias_table(rel_bias):
    p_idx = jnp.arange(BLOCK, dtype=jnp.int32)[:, None]
    r_idx = jnp.arange(BAND, dtype=jnp.int32)[None, :]
    bucket = _t5_bucket(r_idx - BLOCK - p_idx).astype(jnp.int32)
    return pl.pallas_call(
        _bias_table_body,
        out_shape=jax.ShapeDtypeStruct((3, N_HEADS, BLOCK, BAND), F32),
        in_specs=[pl.BlockSpec(memory_space=pltpu.VMEM),
                  pl.BlockSpec(memory_space=pltpu.SMEM)],
        out_specs=pl.BlockSpec(memory_space=pltpu.VMEM),
        name="bias_table",
    )(bucket, rel_bias)


def _inproj_body(x0_ref, xn_ref, g1_ref, w_ref, sgug_ref, qg_ref, kg_ref,
                 ua_ref, vn_ref, q_ref, kv_ref, ga_ref, gb_ref, ha_ref, hb_ref, gv_ref):
    i = pl.program_id(0)
    chunks = D_MODEL // PROJ_COLS

    @pl.when(i == 0)
    def _():
        ha_ref[...] = _rms(x0_ref[...], g1_ref[...]).astype(BF16)

    def head_norm(z, g):
        return jnp.concatenate(
            [_rms(z[:, c:c + HEAD_DIM], g) for c in range(0, z.shape[1], HEAD_DIM)], axis=1)

    def project_rows(h_ref, rows):
        def proj(lo):
            return jnp.dot(h_ref[rows, :], w_ref[:, lo:lo + PROJ_COLS], preferred_element_type=F32)

        ga_lo, gb_lo = 3 * D_MODEL + KV_COLS, 4 * D_MODEL + KV_COLS
        for c in range(chunks):
            cols = slice(c * PROJ_COLS, (c + 1) * PROJ_COLS)
            ua_ref[rows, cols] = _gelu(proj(c * PROJ_COLS)).astype(BF16)
            ga_ref[rows, cols] = _sigmoid(proj(ga_lo + c * PROJ_COLS)).astype(BF16)

        ssq = jnp.zeros((PROJ_ROWS, 1), F32)
        for c in range(chunks):
            cols = slice(c * PROJ_COLS, (c + 1) * PROJ_COLS)
            gv = _gelu(proj(D_MODEL + c * PROJ_COLS))
            gv_ref[rows, cols] = gv
            ssq = ssq + jnp.sum(gv * gv, axis=-1, keepdims=True)
            gb_ref[rows, cols] = _sigmoid(proj(gb_lo + c * PROJ_COLS)).astype(BF16)
        vscale = lax.rsqrt(ssq * (1.0 / D_MODEL) + EPS)
        vn_ref[rows, :] = (gv_ref[rows, :] * vscale * sgug_ref[...]).astype(BF16)

        kcols = N_KV_HEADS * HEAD_DIM
        kv_ref[rows, 0:kcols] = head_norm(proj(3 * D_MODEL), kg_ref[...]).astype(BF16)
        qg = qg_ref[...] * (HEAD_DIM ** -0.5)
        for c in range(chunks):
            cols = slice(c * PROJ_COLS, (c + 1) * PROJ_COLS)
            q_ref[rows, cols] = head_norm(proj(2 * D_MODEL + c * PROJ_COLS), qg).astype(BF16)
        kv_ref[rows, kcols:KV_COLS] = proj(3 * D_MODEL + kcols).astype(BF16)

    def step(h_ref, hnext_ref):
        hnext_ref[...] = _rms(xn_ref[...], g1_ref[...]).astype(BF16)
        for r in range(0, h_ref.shape[0], PROJ_ROWS):
            project_rows(h_ref, slice(r, r + PROJ_ROWS))

    @pl.when(i % 2 == 0)
    def _():
        step(ha_ref, hb_ref)

    @pl.when(i % 2 == 1)
    def _():
        step(hb_ref, ha_ref)


def _inproj(x2d, g1, w_in, sgu_g, q_g, k_g, tile):
    n = x2d.shape[0]
    assert n % tile == 0
    ntiles = n // tile
    row = lambda width: pl.BlockSpec((tile, width), lambda i: (i, 0))
    const = lambda shape: pl.BlockSpec(shape, lambda i: (0,) * len(shape), pipeline_mode=pl.Buffered(1))
    out = lambda width: jax.ShapeDtypeStruct((n, width), BF16)
    x_first = pl.BlockSpec((tile, D_MODEL), lambda i: (0, 0), pipeline_mode=pl.Buffered(1))
    x_next = pl.BlockSpec((tile, D_MODEL), lambda i: (jnp.minimum(i + 1, ntiles - 1), 0))
    return pl.pallas_call(
        _inproj_body,
        grid=(ntiles,),
        in_specs=[x_first, x_next, const((1, D_MODEL)), const((D_MODEL, IN_COLS)),
                  const((1, D_MODEL)), const((1, HEAD_DIM)), const((1, HEAD_DIM))],
        out_specs=[row(D_MODEL), row(D_MODEL), row(D_MODEL), row(KV_COLS), row(D_MODEL), row(D_MODEL)],
        out_shape=[out(D_MODEL), out(D_MODEL), out(D_MODEL), out(KV_COLS), out(D_MODEL), out(D_MODEL)],
        scratch_shapes=[pltpu.VMEM((tile, D_MODEL), BF16), pltpu.VMEM((tile, D_MODEL), BF16),
                        pltpu.VMEM((tile, D_MODEL), F32)],
        compiler_params=pltpu.CompilerParams(dimension_semantics=("arbitrary",),
                                             vmem_limit_bytes=VMEM_LIMIT_BYTES),
        name="inproj",
    )(x2d, x2d, g1, w_in, sgu_g, q_g, k_g)


def _mixer_body(tiles_per_seq, nblk,
                x_ref, ua_ref, vn_ref, q_ref, kv_ref, kvp_ref, kvn_ref, ga_ref, gb_ref,
                tbl_ref, sink_ref, ws_ref, bs_ref, wo_ref, n2g_ref, w1_ref, w2_ref,
                o_ref, band_ref, merged_ref):
    tile = x_ref.shape[0]
    jt = pl.program_id(0) % tiles_per_seq

    band_ref[0:BLOCK, :] = kvp_ref[...]
    band_ref[BLOCK:BLOCK + tile, :] = kv_ref[...]
    band_ref[BLOCK + tile:, :] = kvn_ref[...]

    def block(i):
        rows = slice(i * BLOCK, (i + 1) * BLOCK)
        variant = 1
        if i == 0:
            variant = jnp.where(jt == 0, 0, variant)
        if i == nblk - 1:
            variant = jnp.where(jt == tiles_per_seq - 1, 2, variant)
        for kh in range(N_KV_HEADS):
            kmat = band_ref[i * BLOCK:i * BLOCK + BAND, kh * HEAD_DIM:(kh + 1) * HEAD_DIM]
            vmat = band_ref[i * BLOCK:i * BLOCK + BAND,
                            (N_KV_HEADS + kh) * HEAD_DIM:(N_KV_HEADS + kh + 1) * HEAD_DIM]
            heads = [kh * Q_PER_KV + g for g in range(Q_PER_KV)]
            qcat = jnp.concatenate(
                [q_ref[rows, hd * HEAD_DIM:(hd + 1) * HEAD_DIM] for hd in heads], axis=0)
            s = lax.dot_general(qcat, kmat, (((1,), (1,)), ((), ())), preferred_element_type=F32)
            yield
            probs, denoms = [], []
            for g, hd in enumerate(heads):
                sg = s[g * BLOCK:(g + 1) * BLOCK, :] + tbl_ref[variant, hd]
                sink = sink_ref[hd]
                m = jnp.maximum(jnp.max(sg, axis=-1, keepdims=True), sink)
                e = jnp.exp(sg - m)
                denoms.append(jnp.sum(e, axis=-1, keepdims=True) + jnp.exp(sink - m))
                probs.append(e.astype(BF16))
                yield
            o = jnp.dot(jnp.concatenate(probs, axis=0), vmat, preferred_element_type=F32)
            yield
            for g, hd in enumerate(heads):
                cols = slice(hd * HEAD_DIM, (hd + 1) * HEAD_DIM)
                yb = o[g * BLOCK:(g + 1) * BLOCK, :] / denoms[g]
                mixed = jnp.dot(ws_ref[hd], vn_ref[rows, cols], preferred_element_type=F32) + bs_ref[hd]
                ya = ua_ref[rows, cols].astype(F32) * mixed
                merged = ga_ref[rows, cols].astype(F32) * ya + gb_ref[rows, cols].astype(F32) * yb
                merged_ref[rows, cols] = merged.astype(BF16)
                yield

    def channel_mix(rows):
        x1 = x_ref[rows, :] + jnp.dot(merged_ref[rows, :], wo_ref[...], preferred_element_type=F32)
        h2 = _rms(x1, n2g_ref[...]).astype(BF16)
        yield
        acc = x1
        for c in range(D_FF // D_MODEL):
            sl = slice(c * D_MODEL, (c + 1) * D_MODEL)
            hid = jnp.dot(h2, w1_ref[:, sl], preferred_element_type=F32)
            hid = jnp.square(jnp.maximum(hid, 0.0)).astype(BF16)
            yield
            acc = acc + jnp.dot(hid, w2_ref[sl, :], preferred_element_type=F32)
            yield
        o_ref[rows, :] = acc

    def run(*gens):
        for gen in gens:
            for _ in gen:
                pass

    def interleave(main, filler, per_main):
        for _ in main:
            for _ in range(per_main):
                next(filler, None)
        run(filler)

    per_group = FFN_ROWS // BLOCK
    groups = nblk // per_group

    def mix_group(gi):
        for i in range(gi * per_group, (gi + 1) * per_group):
            yield from block(i)

    def ffn_group(gi):
        return channel_mix(slice(gi * FFN_ROWS, (gi + 1) * FFN_ROWS))

    pieces_per_block = N_KV_HEADS * (2 + 2 * Q_PER_KV)
    ffn_pieces = 1 + 2 * (D_FF // D_MODEL)
    per_main = -(-per_group * pieces_per_block // ffn_pieces)
    run(mix_group(0))
    for gi in range(1, groups):
        interleave(ffn_group(gi - 1), mix_group(gi), per_main)
    run(ffn_group(groups - 1))


def _mixer(x2d, acts, tbl, sink, ws, bs, wo, n2g, w1, w2, seq, tile):
    ua, vn, q, kv, ga, gb = acts
    n = x2d.shape[0]
    assert seq % tile == 0 and tile % FFN_ROWS == 0 and seq >= 2 * BLOCK
    nblk = tile // BLOCK
    tiles_per_seq = seq // tile
    nblocks_total = n // BLOCK
    row = lambda width: pl.BlockSpec((tile, width), lambda j: (j, 0))
    const = lambda shape: pl.BlockSpec(shape, lambda j: (0,) * len(shape), pipeline_mode=pl.Buffered(1))
    prev = pl.BlockSpec((BLOCK, KV_COLS), lambda j: (jnp.maximum(j * nblk - 1, 0), 0))
    nxt = pl.BlockSpec((BLOCK, KV_COLS), lambda j: (jnp.minimum((j + 1) * nblk, nblocks_total - 1), 0))
    return pl.pallas_call(
        functools.partial(_mixer_body, tiles_per_seq, nblk),
        grid=(n // tile,),
        in_specs=[row(D_MODEL), row(D_MODEL), row(D_MODEL), row(D_MODEL), row(KV_COLS), prev, nxt,
                  row(D_MODEL), row(D_MODEL),
                  const((3, N_HEADS, BLOCK, BAND)),
                  pl.BlockSpec(memory_space=pltpu.SMEM),
                  const((SGU_GROUPS, CHUNK, CHUNK)), const((SGU_GROUPS, CHUNK, CHUNK)),
                  const((D_MODEL, D_MODEL)), const((1, D_MODEL)),
                  const((D_MODEL, D_FF)), const((D_FF, D_MODEL))],
        out_specs=row(D_MODEL),
        out_shape=jax.ShapeDtypeStruct((n, D_MODEL), F32),
        scratch_shapes=[pltpu.VMEM((tile + 2 * BLOCK, KV_COLS), BF16),
                        pltpu.VMEM((tile, D_MODEL), BF16)],
        compiler_params=pltpu.CompilerParams(dimension_semantics=("arbitrary",),
                                             vmem_limit_bytes=VMEM_LIMIT_BYTES),
        name="mixer",
    )(x2d, ua, vn, q, kv, kv, kv, ga, gb, tbl, sink, ws, bs, wo, n2g, w1, w2)


def kernel(x_prompt, x_sample, rel_bias, norm1_g, w_in, sgu_norm_g, w_spatial, b_spatial,
           q_norm_g, k_norm_g, sink_logit, w_o, norm2_g, w_ff1, w_ff2):
    assert norm1_g.shape[0] == 1, "single-layer trunk"
    tbl = _bias_table(rel_bias)
    g1 = norm1_g[0].reshape(1, D_MODEL)
    sgu_g = sgu_norm_g[0].reshape(1, D_MODEL)
    q_g = q_norm_g[0].reshape(1, HEAD_DIM)
    k_g = k_norm_g[0].reshape(1, HEAD_DIM)
    n2g = norm2_g[0].reshape(1, D_MODEL)
    w_in_b = w_in[0].astype(BF16)
    ws = w_spatial[0].astype(BF16)
    bs = jnp.broadcast_to(b_spatial[0][:, :, None], (SGU_GROUPS, CHUNK, CHUNK))
    wo = w_o[0].astype(BF16)
    w1 = w_ff1[0].astype(BF16)
    w2 = w_ff2[0].astype(BF16)
    sink = sink_logit[0]

    def trunk(x):
        b, s, d = x.shape
        x2d = x.reshape(b * s, d)
        acts = _inproj(x2d, g1, w_in_b, sgu_g, q_g, k_g, tile=512)
        y = _mixer(x2d, acts, tbl, sink, ws, bs, wo, n2g, w1, w2, seq=s, tile=512)
        return y.reshape(b, s, d)

    return (trunk(x_prompt), trunk(x_sample))
```

```python
import functools
import math

import jax
import jax.numpy as jnp
from jax import lax
from jax.experimental import pallas as pl
from jax.experimental.pallas import tpu as pltpu

D_MODEL = 1024
CHUNK = 128
SGU_GROUPS = 8
N_HEADS = 8
N_KV_HEADS = 2
HEAD_DIM = 128
Q_PER_KV = N_HEADS // N_KV_HEADS
WINDOW = 128
BLOCK = 128
BAND = 3 * BLOCK
N_BUCKETS = 32
MAX_DISTANCE = 128
D_FF = 4 * D_MODEL
EPS = 1e-6
NEG_INF = -1e30
KV_COLS = 2 * N_KV_HEADS * HEAD_DIM
IN_COLS = 2 * D_MODEL + D_MODEL + KV_COLS + 2 * D_MODEL
PROJ_COLS = 256
PROJ_ROWS = 256
FFN_ROWS = 256

F32 = jnp.float32
BF16 = jnp.bfloat16

VMEM_LIMIT_BYTES = 56 * 1024 * 1024


def _t5_bucket(rel):
    nb = N_BUCKETS // 2
    max_exact = nb // 2
    ret = jnp.where(rel > 0, nb, 0)
    n = jnp.abs(rel)
    nf = jnp.maximum(n, 1).astype(F32)
    large = max_exact + (jnp.log(nf / max_exact) / math.log(MAX_DISTANCE / max_exact)
                         * (nb - max_exact)).astype(jnp.int32)
    large = jnp.minimum(large, nb - 1)
    return ret + jnp.where(n < max_exact, n, large)


def _rms(x, g):
    ms = jnp.mean(x * x, axis=-1, keepdims=True)
    return x * lax.rsqrt(ms + EPS) * g


def _gelu(x):
    c = math.sqrt(2.0 / math.pi)
    hx = 0.5 * x
    return hx + hx * jnp.tanh(x * (c + (c * 0.044715) * (x * x)))


def _sigmoid(x):
    return 1.0 / (1.0 + jnp.exp(-x))


def _bias_table_body(bucket_ref, relb_ref, out_ref):
    bucket = bucket_ref[...]
    p = lax.broadcasted_iota(jnp.int32, (BLOCK, BAND), 0)
    r = lax.broadcasted_iota(jnp.int32, (BLOCK, BAND), 1)
    in_win = jnp.abs(r - BLOCK - p) <= WINDOW
    for h in range(N_HEADS):
        acc = jnp.zeros((BLOCK, BAND), F32)
        for b in range(N_BUCKETS):
            acc = jnp.where(bucket == b, relb_ref[b, h], acc)
        base = jnp.where(in_win, acc, NEG_INF)
        out_ref[0, h] = jnp.where(r >= BLOCK, base, NEG_INF)
        out_ref[1, h] = base
        out_ref[2, h] = jnp.where(r < 2 * BLOCK, base, NEG_INF)


def _bias_table(rel_bias):
    p_idx = jnp.arange(BLOCK, dtype=jnp.int32)[:, None]
    r_idx = jnp.arange(BAND, dtype=jnp.int32)[None, :]
    bucket = _t5_bucket(r_idx - BLOCK - p_idx).astype(jnp.int32)
    return pl.pallas_call(
        _bias_table_body,
        out_shape=jax.ShapeDtypeStruct((3, N_HEADS, BLOCK, BAND), F32),
        in_specs=[pl.BlockSpec(memory_space=pltpu.VMEM),
                  pl.BlockSpec(memory_space=pltpu.SMEM)],
        out_specs=pl.BlockSpec(memory_space=pltpu.VMEM),
        name="bias_table",
    )(bucket, rel_bias)


def _inproj_body(lag, x0_ref, xn_ref, g1_ref, w_ref, sgug_ref, qg_ref, kg_ref,
                 ua_ref, vn_ref, q_ref, kv_ref, ga_ref, gb_ref, ha_ref, hb_ref, gv_ref):
    i = pl.program_id(0)
    chunks = D_MODEL // PROJ_COLS

    if lag:
        @pl.when(i == 0)
        def _():
            ha_ref[...] = _rms(x0_ref[...], g1_ref[...]).astype(BF16)

    def head_norm(z, g):
        return jnp.concatenate(
            [_rms(z[:, c:c + HEAD_DIM], g) for c in range(0, z.shape[1], HEAD_DIM)], axis=1)

    def project_rows(h_ref, rows):
        def proj(lo):
            return jnp.dot(h_ref[rows, :], w_ref[:, lo:lo + PROJ_COLS], preferred_element_type=F32)

        ga_lo, gb_lo = 3 * D_MODEL + KV_COLS, 4 * D_MODEL + KV_COLS
        for c in range(chunks):
            cols = slice(c * PROJ_COLS, (c + 1) * PROJ_COLS)
            ua_ref[rows, cols] = _gelu(proj(c * PROJ_COLS)).astype(BF16)
            ga_ref[rows, cols] = _sigmoid(proj(ga_lo + c * PROJ_COLS)).astype(BF16)

        ssq = jnp.zeros((PROJ_ROWS, 1), F32)
        for c in range(chunks):
            cols = slice(c * PROJ_COLS, (c + 1) * PROJ_COLS)
            gv = _gelu(proj(D_MODEL + c * PROJ_COLS))
            gv_ref[rows, cols] = gv
            ssq = ssq + jnp.sum(gv * gv, axis=-1, keepdims=True)
            gb_ref[rows, cols] = _sigmoid(proj(gb_lo + c * PROJ_COLS)).astype(BF16)
        vscale = lax.rsqrt(ssq * (1.0 / D_MODEL) + EPS)
        vn_ref[rows, :] = (gv_ref[rows, :] * vscale * sgug_ref[...]).astype(BF16)

        kcols = N_KV_HEADS * HEAD_DIM
        kv_ref[rows, 0:kcols] = head_norm(proj(3 * D_MODEL), kg_ref[...]).astype(BF16)
        qg = qg_ref[...] * (HEAD_DIM ** -0.5)
        for c in range(chunks):
            cols = slice(c * PROJ_COLS, (c + 1) * PROJ_COLS)
            q_ref[rows, cols] = head_norm(proj(2 * D_MODEL + c * PROJ_COLS), qg).astype(BF16)
        kv_ref[rows, kcols:KV_COLS] = proj(3 * D_MODEL + kcols).astype(BF16)

    def step(h_ref, hnext_ref):
        hnext_ref[...] = _rms(xn_ref[...], g1_ref[...]).astype(BF16)
        for r in range(0, h_ref.shape[0], PROJ_ROWS):
            project_rows(h_ref, slice(r, r + PROJ_ROWS))

    if not lag:
        step(ha_ref, ha_ref)
        return

    @pl.when(i % 2 == 0)
    def _():
        step(ha_ref, hb_ref)

    @pl.when(i % 2 == 1)
    def _():
        step(hb_ref, ha_ref)


def _inproj(x2d, g1, w_in, sgu_g, q_g, k_g, tile, lag):
    n = x2d.shape[0]
    assert n % tile == 0
    ntiles = n // tile
    row = lambda width: pl.BlockSpec((tile, width), lambda i: (i, 0))
    const = lambda shape: pl.BlockSpec(shape, lambda i: (0,) * len(shape), pipeline_mode=pl.Buffered(1))
    out = lambda width: jax.ShapeDtypeStruct((n, width), BF16)
    x_first = pl.BlockSpec((tile, D_MODEL), lambda i: (0, 0), pipeline_mode=pl.Buffered(1))
    x_next = pl.BlockSpec((tile, D_MODEL), lambda i: (jnp.minimum(i + 1, ntiles - 1) if lag else i, 0))
    return pl.pallas_call(
        functools.partial(_inproj_body, lag),
        grid=(ntiles,),
        in_specs=[x_first, x_next, const((1, D_MODEL)), const((D_MODEL, IN_COLS)),
                  const((1, D_MODEL)), const((1, HEAD_DIM)), const((1, HEAD_DIM))],
        out_specs=[row(D_MODEL), row(D_MODEL), row(D_MODEL), row(KV_COLS), row(D_MODEL), row(D_MODEL)],
        out_shape=[out(D_MODEL), out(D_MODEL), out(D_MODEL), out(KV_COLS), out(D_MODEL), out(D_MODEL)],
        scratch_shapes=[pltpu.VMEM((tile, D_MODEL), BF16), pltpu.VMEM((tile, D_MODEL), BF16),
                        pltpu.VMEM((tile, D_MODEL), F32)],
        compiler_params=pltpu.CompilerParams(dimension_semantics=("arbitrary",),
                                             vmem_limit_bytes=VMEM_LIMIT_BYTES),
        name="inproj",
    )(x2d, x2d, g1, w_in, sgu_g, q_g, k_g)


def _mixer_body(tiles_per_seq, nblk,
                x_ref, ua_ref, vn_ref, q_ref, kv_ref, kvp_ref, kvn_ref, ga_ref, gb_ref,
                tbl_ref, sink_ref, ws_ref, bs_ref, wo_ref, n2g_ref, w1_ref, w2_ref,
                o_ref, band_ref, merged_ref):
    tile = x_ref.shape[0]
    jt = pl.program_id(0) % tiles_per_seq

    band_ref[0:BLOCK, :] = kvp_ref[...]
    band_ref[BLOCK:BLOCK + tile, :] = kv_ref[...]
    band_ref[BLOCK + tile:, :] = kvn_ref[...]

    def block(i):
        rows = slice(i * BLOCK, (i + 1) * BLOCK)
        variant = 1
        if i == 0:
            variant = jnp.where(jt == 0, 0, variant)
        if i == nblk - 1:
            variant = jnp.where(jt == tiles_per_seq - 1, 2, variant)
        for kh in range(N_KV_HEADS):
            kmat = band_ref[i * BLOCK:i * BLOCK + BAND, kh * HEAD_DIM:(kh + 1) * HEAD_DIM]
            vmat = band_ref[i * BLOCK:i * BLOCK + BAND,
                            (N_KV_HEADS + kh) * HEAD_DIM:(N_KV_HEADS + kh + 1) * HEAD_DIM]
            heads = [kh * Q_PER_KV + g for g in range(Q_PER_KV)]
            qcat = jnp.concatenate(
                [q_ref[rows, hd * HEAD_DIM:(hd + 1) * HEAD_DIM] for hd in heads], axis=0)
            s = lax.dot_general(qcat, kmat, (((1,), (1,)), ((), ())), preferred_element_type=F32)
            yield
            probs, denoms = [], []
            for g, hd in enumerate(heads):
                sg = s[g * BLOCK:(g + 1) * BLOCK, :] + tbl_ref[variant, hd]
                sink = sink_ref[hd]
                m = jnp.maximum(jnp.max(sg, axis=-1, keepdims=True), sink)
                e = jnp.exp(sg - m)
                denoms.append(jnp.sum(e, axis=-1, keepdims=True) + jnp.exp(sink - m))
                probs.append(e.astype(BF16))
                yield
            o = jnp.dot(jnp.concatenate(probs, axis=0), vmat, preferred_element_type=F32)
            yield
            for g, hd in enumerate(heads):
                cols = slice(hd * HEAD_DIM, (hd + 1) * HEAD_DIM)
                yb = o[g * BLOCK:(g + 1) * BLOCK, :] / denoms[g]
                mixed = jnp.dot(ws_ref[hd], vn_ref[rows, cols], preferred_element_type=F32) + bs_ref[hd]
                ya = ua_ref[rows, cols].astype(F32) * mixed
                merged = ga_ref[rows, cols].astype(F32) * ya + gb_ref[rows, cols].astype(F32) * yb
                merged_ref[rows, cols] = merged.astype(BF16)
                yield

    def channel_mix(rows):
        x1 = x_ref[rows, :] + jnp.dot(merged_ref[rows, :], wo_ref[...], preferred_element_type=F32)
        h2 = _rms(x1, n2g_ref[...]).astype(BF16)
        yield
        acc = x1
        for c in range(D_FF // D_MODEL):
            sl = slice(c * D_MODEL, (c + 1) * D_MODEL)
            hid = jnp.dot(h2, w1_ref[:, sl], preferred_element_type=F32)
            hid = jnp.square(jnp.maximum(hid, 0.0)).astype(BF16)
            yield
            acc = acc + jnp.dot(hid, w2_ref[sl, :], preferred_element_type=F32)
            yield
        o_ref[rows, :] = acc

    def run(*gens):
        for gen in gens:
            for _ in gen:
                pass

    def interleave(main, filler, per_main):
        for _ in main:
            for _ in range(per_main):
                next(filler, None)
        run(filler)

    per_group = FFN_ROWS // BLOCK
    groups = nblk // per_group

    def mix_group(gi):
        for i in range(gi * per_group, (gi + 1) * per_group):
            yield from block(i)

    def ffn_group(gi):
        return channel_mix(slice(gi * FFN_ROWS, (gi + 1) * FFN_ROWS))

    pieces_per_block = N_KV_HEADS * (2 + 2 * Q_PER_KV)
    ffn_pieces = 1 + 2 * (D_FF // D_MODEL)
    per_main = -(-per_group * pieces_per_block // ffn_pieces)
    run(mix_group(0))
    for gi in range(1, groups):
        interleave(ffn_group(gi - 1), mix_group(gi), per_main)
    run(ffn_group(groups - 1))


def _mixer(x2d, acts, tbl, sink, ws, bs, wo, n2g, w1, w2, seq, tile):
    ua, vn, q, kv, ga, gb = acts
    n = x2d.shape[0]
    assert seq % tile == 0 and tile % FFN_ROWS == 0 and seq >= 2 * BLOCK
    nblk = tile // BLOCK
    tiles_per_seq = seq // tile
    nblocks_total = n // BLOCK
    row = lambda width: pl.BlockSpec((tile, width), lambda j: (j, 0))
    const = lambda shape: pl.BlockSpec(shape, lambda j: (0,) * len(shape), pipeline_mode=pl.Buffered(1))
    prev = pl.BlockSpec((BLOCK, KV_COLS), lambda j: (jnp.maximum(j * nblk - 1, 0), 0))
    nxt = pl.BlockSpec((BLOCK, KV_COLS), lambda j: (jnp.minimum((j + 1) * nblk, nblocks_total - 1), 0))
    return pl.pallas_call(
        functools.partial(_mixer_body, tiles_per_seq, nblk),
        grid=(n // tile,),
        in_specs=[row(D_MODEL), row(D_MODEL), row(D_MODEL), row(D_MODEL), row(KV_COLS), prev, nxt,
                  row(D_MODEL), row(D_MODEL),
                  const((3, N_HEADS, BLOCK, BAND)),
                  pl.BlockSpec(memory_space=pltpu.SMEM),
                  const((SGU_GROUPS, CHUNK, CHUNK)), const((SGU_GROUPS, CHUNK, CHUNK)),
                  const((D_MODEL, D_MODEL)), const((1, D_MODEL)),
                  const((D_MODEL, D_FF)), const((D_FF, D_MODEL))],
        out_specs=row(D_MODEL),
        out_shape=jax.ShapeDtypeStruct((n, D_MODEL), F32),
        scratch_shapes=[pltpu.VMEM((tile + 2 * BLOCK, KV_COLS), BF16),
                        pltpu.VMEM((tile, D_MODEL), BF16)],
        compiler_params=pltpu.CompilerParams(dimension_semantics=("arbitrary",),
                                             vmem_limit_bytes=VMEM_LIMIT_BYTES),
        name="mixer",
    )(x2d, ua, vn, q, kv, kv, kv, ga, gb, tbl, sink, ws, bs, wo, n2g, w1, w2)


def kernel(x_prompt, x_sample, rel_bias, norm1_g, w_in, sgu_norm_g, w_spatial, b_spatial,
           q_norm_g, k_norm_g, sink_logit, w_o, norm2_g, w_ff1, w_ff2):
    assert norm1_g.shape[0] == 1, "single-layer trunk"
    tbl = _bias_table(rel_bias)
    g1 = norm1_g[0].reshape(1, D_MODEL)
    sgu_g = sgu_norm_g[0].reshape(1, D_MODEL)
    q_g = q_norm_g[0].reshape(1, HEAD_DIM)
    k_g = k_norm_g[0].reshape(1, HEAD_DIM)
    n2g = norm2_g[0].reshape(1, D_MODEL)
    w_in_b = w_in[0].astype(BF16)
    ws = w_spatial[0].astype(BF16)
    bs = jnp.broadcast_to(b_spatial[0][:, :, None], (SGU_GROUPS, CHUNK, CHUNK))
    wo = w_o[0].astype(BF16)
    w1 = w_ff1[0].astype(BF16)
    w2 = w_ff2[0].astype(BF16)
    sink = sink_logit[0]

    def trunk(x):
        b, s, d = x.shape
        x2d = x.reshape(b * s, d)
        if s == 4096:
            acts = _inproj(x2d, g1, w_in_b, sgu_g, q_g, k_g, tile=256, lag=True)
        else:
            acts = _inproj(x2d, g1, w_in_b, sgu_g, q_g, k_g, tile=512, lag=False)
        y = _mixer(x2d, acts, tbl, sink, ws, bs, wo, n2g, w1, w2, seq=s, tile=512)
        return y.reshape(b, s, d)

    return (trunk(x_prompt), trunk(x_sample))
```

```python
import functools
import math

import jax
import jax.numpy as jnp
from jax import lax
from jax.experimental import pallas as pl
from jax.experimental.pallas import tpu as pltpu

D_MODEL = 1024
CHUNK = 128
SGU_GROUPS = 8
N_HEADS = 8
N_KV_HEADS = 2
HEAD_DIM = 128
Q_PER_KV = N_HEADS // N_KV_HEADS
WINDOW = 128
BLOCK = 128
BAND = 3 * BLOCK
N_BUCKETS = 32
MAX_DISTANCE = 128
D_FF = 4 * D_MODEL
EPS = 1e-6
NEG_INF = -1e30
KV_COLS = 2 * N_KV_HEADS * HEAD_DIM
IN_COLS = 2 * D_MODEL + D_MODEL + KV_COLS + 2 * D_MODEL
PROJ_COLS = 256
PROJ_ROWS = 256
FFN_ROWS = 256

F32 = jnp.float32
BF16 = jnp.bfloat16

VMEM_LIMIT_BYTES = 56 * 1024 * 1024


def _t5_bucket(rel):
    nb = N_BUCKETS // 2
    max_exact = nb // 2
    ret = jnp.where(rel > 0, nb, 0)
    n = jnp.abs(rel)
    nf = jnp.maximum(n, 1).astype(F32)
    large = max_exact + (jnp.log(nf / max_exact) / math.log(MAX_DISTANCE / max_exact)
                         * (nb - max_exact)).astype(jnp.int32)
    large = jnp.minimum(large, nb - 1)
    return ret + jnp.where(n < max_exact, n, large)


def _rms(x, g):
    ms = jnp.mean(x * x, axis=-1, keepdims=True)
    return x * lax.rsqrt(ms + EPS) * g


def _gelu(x):
    c = math.sqrt(2.0 / math.pi)
    hx = 0.5 * x
    return hx + hx * jnp.tanh(x * (c + (c * 0.044715) * (x * x)))


def _sigmoid(x):
    return 1.0 / (1.0 + jnp.exp(-x))


def _bias_table_body(bucket_ref, relb_ref, out_ref):
    bucket = bucket_ref[...]
    p = lax.broadcasted_iota(jnp.int32, (BLOCK, BAND), 0)
    r = lax.broadcasted_iota(jnp.int32, (BLOCK, BAND), 1)
    in_win = jnp.abs(r - BLOCK - p) <= WINDOW
    for h in range(N_HEADS):
        acc = jnp.zeros((BLOCK, BAND), F32)
        for b in range(N_BUCKETS):
            acc = jnp.where(bucket == b, relb_ref[b, h], acc)
        base = jnp.where(in_win, acc, NEG_INF)
        out_ref[0, h] = jnp.where(r >= BLOCK, base, NEG_INF)
        out_ref[1, h] = base
        out_ref[2, h] = jnp.where(r < 2 * BLOCK, base, NEG_INF)


def _bias_table(rel_bias):
    p_idx = jnp.arange(BLOCK, dtype=jnp.int32)[:, None]
    r_idx = jnp.arange(BAND, dtype=jnp.int32)[None, :]
    bucket = _t5_bucket(r_idx - BLOCK - p_idx).astype(jnp.int32)
    return pl.pallas_call(
        _bias_table_body,
        out_shape=jax.ShapeDtypeStruct((3, N_HEADS, BLOCK, BAND), F32),
        in_specs=[pl.BlockSpec(memory_space=pltpu.VMEM),
                  pl.BlockSpec(memory_space=pltpu.SMEM)],
        out_specs=pl.BlockSpec(memory_space=pltpu.VMEM),
        name="bias_table",
    )(bucket, rel_bias)


def _inproj_body(x_ref, g1_ref, w_ref, sgug_ref, qg_ref, kg_ref,
                 ua_ref, vn_ref, q_ref, kv_ref, ga_ref, gb_ref, h_ref, gv_ref):
    h_ref[...] = _rms(x_ref[...], g1_ref[...]).astype(BF16)
    chunks = D_MODEL // PROJ_COLS

    def head_norm(z, g):
        return jnp.concatenate(
            [_rms(z[:, c:c + HEAD_DIM], g) for c in range(0, z.shape[1], HEAD_DIM)], axis=1)

    def project_rows(rows):
        def proj(lo):
            return jnp.dot(h_ref[rows, :], w_ref[:, lo:lo + PROJ_COLS], preferred_element_type=F32)

        ga_lo, gb_lo = 3 * D_MODEL + KV_COLS, 4 * D_MODEL + KV_COLS
        for c in range(chunks):
            cols = slice(c * PROJ_COLS, (c + 1) * PROJ_COLS)
            ua_ref[rows, cols] = _gelu(proj(c * PROJ_COLS)).astype(BF16)
            ga_ref[rows, cols] = _sigmoid(proj(ga_lo + c * PROJ_COLS)).astype(BF16)

        ssq = jnp.zeros((PROJ_ROWS, 1), F32)
        for c in range(chunks):
            cols = slice(c * PROJ_COLS, (c + 1) * PROJ_COLS)
            gv = _gelu(proj(D_MODEL + c * PROJ_COLS))
            gv_ref[rows, cols] = gv
            ssq = ssq + jnp.sum(gv * gv, axis=-1, keepdims=True)
            gb_ref[rows, cols] = _sigmoid(proj(gb_lo + c * PROJ_COLS)).astype(BF16)
        vscale = lax.rsqrt(ssq * (1.0 / D_MODEL) + EPS)
        vn_ref[rows, :] = (gv_ref[rows, :] * vscale * sgug_ref[...]).astype(BF16)

        kcols = N_KV_HEADS * HEAD_DIM
        kv_ref[rows, 0:kcols] = head_norm(proj(3 * D_MODEL), kg_ref[...]).astype(BF16)
        qg = qg_ref[...] * (HEAD_DIM ** -0.5)
        for c in range(chunks):
            cols = slice(c * PROJ_COLS, (c + 1) * PROJ_COLS)
            q_ref[rows, cols] = head_norm(proj(2 * D_MODEL + c * PROJ_COLS), qg).astype(BF16)
        kv_ref[rows, kcols:KV_COLS] = proj(3 * D_MODEL + kcols).astype(BF16)

    for r in range(0, x_ref.shape[0], PROJ_ROWS):
        project_rows(slice(r, r + PROJ_ROWS))


def _inproj(x2d, g1, w_in, sgu_g, q_g, k_g, tile):
    n = x2d.shape[0]
    assert n % tile == 0 and tile % PROJ_ROWS == 0
    row = lambda width: pl.BlockSpec((tile, width), lambda i: (i, 0))
    const = lambda shape: pl.BlockSpec(shape, lambda i: (0,) * len(shape), pipeline_mode=pl.Buffered(1))
    out = lambda width: jax.ShapeDtypeStruct((n, width), BF16)
    return pl.pallas_call(
        _inproj_body,
        grid=(n // tile,),
        in_specs=[row(D_MODEL), const((1, D_MODEL)), const((D_MODEL, IN_COLS)),
                  const((1, D_MODEL)), const((1, HEAD_DIM)), const((1, HEAD_DIM))],
        out_specs=[row(D_MODEL), row(D_MODEL), row(D_MODEL), row(KV_COLS), row(D_MODEL), row(D_MODEL)],
        out_shape=[out(D_MODEL), out(D_MODEL), out(D_MODEL), out(KV_COLS), out(D_MODEL), out(D_MODEL)],
        scratch_shapes=[pltpu.VMEM((tile, D_MODEL), BF16), pltpu.VMEM((tile, D_MODEL), F32)],
        compiler_params=pltpu.CompilerParams(dimension_semantics=("arbitrary",),
                                             vmem_limit_bytes=VMEM_LIMIT_BYTES),
        name="inproj",
    )(x2d, g1, w_in, sgu_g, q_g, k_g)


def _mixer_body(tiles_per_seq, nblk,
                x_ref, ua_ref, vn_ref, q_ref, kv_ref, kvp_ref, kvn_ref, ga_ref, gb_ref,
                tbl_ref, sink_ref, ws_ref, bs_ref, wo_ref, n2g_ref, w1_ref, w2_ref,
                o_ref, band_ref, merged_ref):
    tile = x_ref.shape[0]
    jt = pl.program_id(0) % tiles_per_seq

    band_ref[0:BLOCK, :] = kvp_ref[...]
    band_ref[BLOCK:BLOCK + tile, :] = kv_ref[...]
    band_ref[BLOCK + tile:, :] = kvn_ref[...]

    def block(i):
        rows = slice(i * BLOCK, (i + 1) * BLOCK)
        variant = 1
        if i == 0:
            variant = jnp.where(jt == 0, 0, variant)
        if i == nblk - 1:
            variant = jnp.where(jt == tiles_per_seq - 1, 2, variant)
        for kh in range(N_KV_HEADS):
            kmat = band_ref[i * BLOCK:i * BLOCK + BAND, kh * HEAD_DIM:(kh + 1) * HEAD_DIM]
            vmat = band_ref[i * BLOCK:i * BLOCK + BAND,
                            (N_KV_HEADS + kh) * HEAD_DIM:(N_KV_HEADS + kh + 1) * HEAD_DIM]
            heads = [kh * Q_PER_KV + g for g in range(Q_PER_KV)]
            qcat = jnp.concatenate(
                [q_ref[rows, hd * HEAD_DIM:(hd + 1) * HEAD_DIM] for hd in heads], axis=0)
            s = lax.dot_general(qcat, kmat, (((1,), (1,)), ((), ())), preferred_element_type=F32)
            yield
            probs, denoms = [], []
            for g, hd in enumerate(heads):
                sg = s[g * BLOCK:(g + 1) * BLOCK, :] + tbl_ref[variant, hd]
                sink = sink_ref[hd]
                m = jnp.maximum(jnp.max(sg, axis=-1, keepdims=True), sink)
                e = jnp.exp(sg - m)
                denoms.append(jnp.sum(e, axis=-1, keepdims=True) + jnp.exp(sink - m))
                probs.append(e.astype(BF16))
                yield
            o = jnp.dot(jnp.concatenate(probs, axis=0), vmat, preferred_element_type=F32)
            yield
            for g, hd in enumerate(heads):
                cols = slice(hd * HEAD_DIM, (hd + 1) * HEAD_DIM)
                yb = o[g * BLOCK:(g + 1) * BLOCK, :] / denoms[g]
                mixed = jnp.dot(ws_ref[hd], vn_ref[rows, cols], preferred_element_type=F32) + bs_ref[hd]
                ya = ua_ref[rows, cols].astype(F32) * mixed
                merged = ga_ref[rows, cols].astype(F32) * ya + gb_ref[rows, cols].astype(F32) * yb
                merged_ref[rows, cols] = merged.astype(BF16)
                yield

    def channel_mix(rows):
        x1 = x_ref[rows, :] + jnp.dot(merged_ref[rows, :], wo_ref[...], preferred_element_type=F32)
        h2 = _rms(x1, n2g_ref[...]).astype(BF16)
        yield
        acc = x1
        for c in range(D_FF // D_MODEL):
            sl = slice(c * D_MODEL, (c + 1) * D_MODEL)
            hid = jnp.dot(h2, w1_ref[:, sl], preferred_element_type=F32)
            hid = jnp.square(jnp.maximum(hid, 0.0)).astype(BF16)
            yield
            acc = acc + jnp.dot(hid, w2_ref[sl, :], preferred_element_type=F32)
            yield
        o_ref[rows, :] = acc

    def run(*gens):
        for gen in gens:
            for _ in gen:
                pass

    def interleave(main, filler, per_main):
        for _ in main:
            for _ in range(per_main):
                next(filler, None)
        run(filler)

    per_group = FFN_ROWS // BLOCK
    groups = nblk // per_group

    def mix_group(gi):
        for i in range(gi * per_group, (gi + 1) * per_group):
            yield from block(i)

    def ffn_group(gi):
        return channel_mix(slice(gi * FFN_ROWS, (gi + 1) * FFN_ROWS))

    pieces_per_block = N_KV_HEADS * (2 + 2 * Q_PER_KV)
    ffn_pieces = 1 + 2 * (D_FF // D_MODEL)
    per_main = -(-per_group * pieces_per_block // ffn_pieces)
    run(mix_group(0))
    for gi in range(1, groups):
        interleave(ffn_group(gi - 1), mix_group(gi), per_main)
    run(ffn_group(groups - 1))


def _mixer(x2d, acts, tbl, sink, ws, bs, wo, n2g, w1, w2, seq, tile):
    ua, vn, q, kv, ga, gb = acts
    n = x2d.shape[0]
    assert seq % tile == 0 and tile % FFN_ROWS == 0 and seq >= 2 * BLOCK
    nblk = tile // BLOCK
    tiles_per_seq = seq // tile
    nblocks_total = n // BLOCK
    row = lambda width: pl.BlockSpec((tile, width), lambda j: (j, 0))
    const = lambda shape: pl.BlockSpec(shape, lambda j: (0,) * len(shape), pipeline_mode=pl.Buffered(1))
    prev = pl.BlockSpec((BLOCK, KV_COLS), lambda j: (jnp.maximum(j * nblk - 1, 0), 0))
    nxt = pl.BlockSpec((BLOCK, KV_COLS), lambda j: (jnp.minimum((j + 1) * nblk, nblocks_total - 1), 0))
    return pl.pallas_call(
        functools.partial(_mixer_body, tiles_per_seq, nblk),
        grid=(n // tile,),
        in_specs=[row(D_MODEL), row(D_MODEL), row(D_MODEL), row(D_MODEL), row(KV_COLS), prev, nxt,
                  row(D_MODEL), row(D_MODEL),
                  const((3, N_HEADS, BLOCK, BAND)),
                  pl.BlockSpec(memory_space=pltpu.SMEM),
                  const((SGU_GROUPS, CHUNK, CHUNK)), const((SGU_GROUPS, CHUNK, CHUNK)),
                  const((D_MODEL, D_MODEL)), const((1, D_MODEL)),
                  const((D_MODEL, D_FF)), const((D_FF, D_MODEL))],
        out_specs=row(D_MODEL),
        out_shape=jax.ShapeDtypeStruct((n, D_MODEL), F32),
        scratch_shapes=[pltpu.VMEM((tile + 2 * BLOCK, KV_COLS), BF16),
                        pltpu.VMEM((tile, D_MODEL), BF16)],
        compiler_params=pltpu.CompilerParams(dimension_semantics=("arbitrary",),
                                             vmem_limit_bytes=VMEM_LIMIT_BYTES),
        name="mixer",
    )(x2d, ua, vn, q, kv, kv, kv, ga, gb, tbl, sink, ws, bs, wo, n2g, w1, w2)


def kernel(x_prompt, x_sample, rel_bias, norm1_g, w_in, sgu_norm_g, w_spatial, b_spatial,
           q_norm_g, k_norm_g, sink_logit, w_o, norm2_g, w_ff1, w_ff2):
    assert norm1_g.shape[0] == 1, "single-layer trunk"
    tbl = _bias_table(rel_bias)
    g1 = norm1_g[0].reshape(1, D_MODEL)
    sgu_g = sgu_norm_g[0].reshape(1, D_MODEL)
    q_g = q_norm_g[0].reshape(1, HEAD_DIM)
    k_g = k_norm_g[0].reshape(1, HEAD_DIM)
    n2g = norm2_g[0].reshape(1, D_MODEL)
    w_in_b = w_in[0].astype(BF16)
    ws = w_spatial[0].astype(BF16)
    bs = jnp.broadcast_to(b_spatial[0][:, :, None], (SGU_GROUPS, CHUNK, CHUNK))
    wo = w_o[0].astype(BF16)
    w1 = w_ff1[0].astype(BF16)
    w2 = w_ff2[0].astype(BF16)
    sink = sink_logit[0]

    def trunk(x):
        b, s, d = x.shape
        x2d = x.reshape(b * s, d)
        acts = _inproj(x2d, g1, w_in_b, sgu_g, q_g, k_g, tile=512)
        y = _mixer(x2d, acts, tbl, sink, ws, bs, wo, n2g, w1, w2, seq=s, tile=512)
        return y.reshape(b, s, d)

    return (trunk(x_prompt), trunk(x_sample))
```

```python
import functools
import math

import jax
import jax.numpy as jnp
from jax import lax
from jax.experimental import pallas as pl
from jax.experimental.pallas import tpu as pltpu

D_MODEL = 1024
CHUNK = 128
SGU_GROUPS = 8
N_HEADS = 8
N_KV_HEADS = 2
HEAD_DIM = 128
Q_PER_KV = N_HEADS // N_KV_HEADS
WINDOW = 128
BLOCK = 128
BAND = 3 * BLOCK
N_BUCKETS = 32
MAX_DISTANCE = 128
D_FF = 4 * D_MODEL
EPS = 1e-6
NEG_INF = -1e30
LOG2E = math.log2(math.e)
KV_COLS = 2 * N_KV_HEADS * HEAD_DIM
IN_COLS = 2 * D_MODEL + D_MODEL + KV_COLS + 2 * D_MODEL
PROJ_COLS = 256
PROJ_ROWS = 256
FFN_ROWS = 256

F32 = jnp.float32
BF16 = jnp.bfloat16

VMEM_LIMIT_BYTES = 56 * 1024 * 1024


def _t5_bucket(rel):
    nb = N_BUCKETS // 2
    max_exact = nb // 2
    ret = jnp.where(rel > 0, nb, 0)
    n = jnp.abs(rel)
    nf = jnp.maximum(n, 1).astype(F32)
    large = max_exact + (jnp.log(nf / max_exact) / math.log(MAX_DISTANCE / max_exact)
                         * (nb - max_exact)).astype(jnp.int32)
    large = jnp.minimum(large, nb - 1)
    return ret + jnp.where(n < max_exact, n, large)


def _rms(x, g):
    ms = jnp.mean(x * x, axis=-1, keepdims=True)
    return x * lax.rsqrt(ms + EPS) * g


def _gelu(x):
    c = math.sqrt(2.0 / math.pi)
    hx = 0.5 * x
    return hx + hx * jnp.tanh(x * (c + (c * 0.044715) * (x * x)))


def _sigmoid(x):
    return 1.0 / (1.0 + jnp.exp2(x * (-LOG2E)))


def _bias_table_body(bucket_ref, relb_ref, out_ref):
    bucket = bucket_ref[...]
    p = lax.broadcasted_iota(jnp.int32, (BLOCK, BAND), 0)
    r = lax.broadcasted_iota(jnp.int32, (BLOCK, BAND), 1)
    in_win = jnp.abs(r - BLOCK - p) <= WINDOW
    for h in range(N_HEADS):
        acc = jnp.zeros((BLOCK, BAND), F32)
        for b in range(N_BUCKETS):
            acc = jnp.where(bucket == b, relb_ref[b, h], acc)
        base = jnp.where(in_win, acc * LOG2E, NEG_INF)
        out_ref[0, h] = jnp.where(r >= BLOCK, base, NEG_INF)
        out_ref[1, h] = base
        out_ref[2, h] = jnp.where(r < 2 * BLOCK, base, NEG_INF)


def _bias_table(rel_bias):
    p_idx = jnp.arange(BLOCK, dtype=jnp.int32)[:, None]
    r_idx = jnp.arange(BAND, dtype=jnp.int32)[None, :]
    bucket = _t5_bucket(r_idx - BLOCK - p_idx).astype(jnp.int32)
    return pl.pallas_call(
        _bias_table_body,
        out_shape=jax.ShapeDtypeStruct((3, N_HEADS, BLOCK, BAND), F32),
        in_specs=[pl.BlockSpec(memory_space=pltpu.VMEM),
                  pl.BlockSpec(memory_space=pltpu.SMEM)],
        out_specs=pl.BlockSpec(memory_space=pltpu.VMEM),
        name="bias_table",
    )(bucket, rel_bias)


def _inproj_body(x_ref, g1_ref, w_ref, sgug_ref, qg_ref, kg_ref,
                 ua_ref, vn_ref, q_ref, kv_ref, ga_ref, gb_ref, h_ref, gv_ref):
    h_ref[...] = _rms(x_ref[...], g1_ref[...]).astype(BF16)
    chunks = D_MODEL // PROJ_COLS

    def head_norm(z, g):
        return jnp.concatenate(
            [_rms(z[:, c:c + HEAD_DIM], g) for c in range(0, z.shape[1], HEAD_DIM)], axis=1)

    def project_rows(rows):
        def proj(lo):
            return jnp.dot(h_ref[rows, :], w_ref[:, lo:lo + PROJ_COLS], preferred_element_type=F32)

        ga_lo, gb_lo = 3 * D_MODEL + KV_COLS, 4 * D_MODEL + KV_COLS
        for c in range(chunks):
            cols = slice(c * PROJ_COLS, (c + 1) * PROJ_COLS)
            ua_ref[rows, cols] = _gelu(proj(c * PROJ_COLS)).astype(BF16)
            ga_ref[rows, cols] = _sigmoid(proj(ga_lo + c * PROJ_COLS)).astype(BF16)

        ssq = jnp.zeros((PROJ_ROWS, 1), F32)
        for c in range(chunks):
            cols = slice(c * PROJ_COLS, (c + 1) * PROJ_COLS)
            gv = _gelu(proj(D_MODEL + c * PROJ_COLS))
            gv_ref[rows, cols] = gv
            ssq = ssq + jnp.sum(gv * gv, axis=-1, keepdims=True)
            gb_ref[rows, cols] = _sigmoid(proj(gb_lo + c * PROJ_COLS)).astype(BF16)
        vscale = lax.rsqrt(ssq * (1.0 / D_MODEL) + EPS)
        vn_ref[rows, :] = (gv_ref[rows, :] * vscale * sgug_ref[...]).astype(BF16)

        kcols = N_KV_HEADS * HEAD_DIM
        kv_ref[rows, 0:kcols] = head_norm(proj(3 * D_MODEL), kg_ref[...]).astype(BF16)
        qg = qg_ref[...] * (HEAD_DIM ** -0.5 * LOG2E)
        for c in range(chunks):
            cols = slice(c * PROJ_COLS, (c + 1) * PROJ_COLS)
            q_ref[rows, cols] = head_norm(proj(2 * D_MODEL + c * PROJ_COLS), qg).astype(BF16)
        kv_ref[rows, kcols:KV_COLS] = proj(3 * D_MODEL + kcols).astype(BF16)

    for r in range(0, x_ref.shape[0], PROJ_ROWS):
        project_rows(slice(r, r + PROJ_ROWS))


def _inproj(x2d, g1, w_in, sgu_g, q_g, k_g, tile):
    n = x2d.shape[0]
    assert n % tile == 0 and tile % PROJ_ROWS == 0
    row = lambda width: pl.BlockSpec((tile, width), lambda i: (i, 0))
    const = lambda shape: pl.BlockSpec(shape, lambda i: (0,) * len(shape), pipeline_mode=pl.Buffered(1))
    out = lambda width: jax.ShapeDtypeStruct((n, width), BF16)
    return pl.pallas_call(
        _inproj_body,
        grid=(n // tile,),
        in_specs=[row(D_MODEL), const((1, D_MODEL)), const((D_MODEL, IN_COLS)),
                  const((1, D_MODEL)), const((1, HEAD_DIM)), const((1, HEAD_DIM))],
        out_specs=[row(D_MODEL), row(D_MODEL), row(D_MODEL), row(KV_COLS), row(D_MODEL), row(D_MODEL)],
        out_shape=[out(D_MODEL), out(D_MODEL), out(D_MODEL), out(KV_COLS), out(D_MODEL), out(D_MODEL)],
        scratch_shapes=[pltpu.VMEM((tile, D_MODEL), BF16), pltpu.VMEM((tile, D_MODEL), F32)],
        compiler_params=pltpu.CompilerParams(dimension_semantics=("arbitrary",),
                                             vmem_limit_bytes=VMEM_LIMIT_BYTES),
        name="inproj",
    )(x2d, g1, w_in, sgu_g, q_g, k_g)


def _mixer_body(tiles_per_seq, nblk,
                x_ref, ua_ref, vn_ref, q_ref, kv_ref, kvp_ref, kvn_ref, ga_ref, gb_ref,
                tbl_ref, sink_ref, ws_ref, bs_ref, wo_ref, n2g_ref, w1_ref, w2_ref,
                o_ref, band_ref, merged_ref):
    tile = x_ref.shape[0]
    jt = pl.program_id(0) % tiles_per_seq

    band_ref[0:BLOCK, :] = kvp_ref[...]
    band_ref[BLOCK:BLOCK + tile, :] = kv_ref[...]
    band_ref[BLOCK + tile:, :] = kvn_ref[...]

    def block(i):
        rows = slice(i * BLOCK, (i + 1) * BLOCK)
        variant = 1
        if i == 0:
            variant = jnp.where(jt == 0, 0, variant)
        if i == nblk - 1:
            variant = jnp.where(jt == tiles_per_seq - 1, 2, variant)
        for kh in range(N_KV_HEADS):
            kmat = band_ref[i * BLOCK:i * BLOCK + BAND, kh * HEAD_DIM:(kh + 1) * HEAD_DIM]
            vmat = band_ref[i * BLOCK:i * BLOCK + BAND,
                            (N_KV_HEADS + kh) * HEAD_DIM:(N_KV_HEADS + kh + 1) * HEAD_DIM]
            heads = [kh * Q_PER_KV + g for g in range(Q_PER_KV)]
            qcat = jnp.concatenate(
                [q_ref[rows, hd * HEAD_DIM:(hd + 1) * HEAD_DIM] for hd in heads], axis=0)
            s = lax.dot_general(qcat, kmat, (((1,), (1,)), ((), ())), preferred_element_type=F32)
            yield
            probs, denoms = [], []
            for g, hd in enumerate(heads):
                sg = s[g * BLOCK:(g + 1) * BLOCK, :] + tbl_ref[variant, hd]
                sink = sink_ref[hd] * LOG2E
                m = jnp.maximum(jnp.max(sg, axis=-1, keepdims=True), sink)
                e = jnp.exp2(sg - m)
                denoms.append(jnp.sum(e, axis=-1, keepdims=True) + jnp.exp2(sink - m))
                probs.append(e.astype(BF16))
                yield
            o = jnp.dot(jnp.concatenate(probs, axis=0), vmat, preferred_element_type=F32)
            yield
            for g, hd in enumerate(heads):
                cols = slice(hd * HEAD_DIM, (hd + 1) * HEAD_DIM)
                yb = (o[g * BLOCK:(g + 1) * BLOCK, :] / denoms[g]).astype(BF16)
                mixed = jnp.dot(ws_ref[hd], vn_ref[rows, cols], preferred_element_type=F32) + bs_ref[hd]
                ya = ua_ref[rows, cols] * mixed.astype(BF16)
                merged_ref[rows, cols] = ga_ref[rows, cols] * ya + gb_ref[rows, cols] * yb
                yield

    def channel_mix(rows):
        x1 = x_ref[rows, :] + jnp.dot(merged_ref[rows, :], wo_ref[...], preferred_element_type=F32)
        h2 = _rms(x1, n2g_ref[...]).astype(BF16)
        yield
        acc = x1
        for c in range(D_FF // D_MODEL):
            sl = slice(c * D_MODEL, (c + 1) * D_MODEL)
            hid = jnp.dot(h2, w1_ref[:, sl], preferred_element_type=F32)
            hid = jnp.square(jnp.maximum(hid, 0.0)).astype(BF16)
            yield
            acc = acc + jnp.dot(hid, w2_ref[sl, :], preferred_element_type=F32)
            yield
        o_ref[rows, :] = acc

    def run(*gens):
        for gen in gens:
            for _ in gen:
                pass

    def interleave(main, filler, per_main):
        for _ in main:
            for _ in range(per_main):
                next(filler, None)
        run(filler)

    per_group = FFN_ROWS // BLOCK
    groups = nblk // per_group

    def mix_group(gi):
        live = [block(i) for i in range(gi * per_group, (gi + 1) * per_group)]
        while live:
            for gen in list(live):
                try:
                    next(gen)
                    yield
                except StopIteration:
                    live.remove(gen)

    def ffn_group(gi):
        return channel_mix(slice(gi * FFN_ROWS, (gi + 1) * FFN_ROWS))

    pieces_per_block = N_KV_HEADS * (2 + 2 * Q_PER_KV)
    ffn_pieces = 1 + 2 * (D_FF // D_MODEL)
    per_main = -(-per_group * pieces_per_block // ffn_pieces)
    run(mix_group(0))
    for gi in range(1, groups):
        interleave(ffn_group(gi - 1), mix_group(gi), per_main)
    run(ffn_group(groups - 1))


def _mixer(x2d, acts, tbl, sink, ws, bs, wo, n2g, w1, w2, seq, tile):
    ua, vn, q, kv, ga, gb = acts
    n = x2d.shape[0]
    assert seq % tile == 0 and tile % FFN_ROWS == 0 and seq >= 2 * BLOCK
    nblk = tile // BLOCK
    tiles_per_seq = seq // tile
    nblocks_total = n // BLOCK
    row = lambda width: pl.BlockSpec((tile, width), lambda j: (j, 0))
    const = lambda shape: pl.BlockSpec(shape, lambda j: (0,) * len(shape), pipeline_mode=pl.Buffered(1))
    prev = pl.BlockSpec((BLOCK, KV_COLS), lambda j: (jnp.maximum(j * nblk - 1, 0), 0))
    nxt = pl.BlockSpec((BLOCK, KV_COLS), lambda j: (jnp.minimum((j + 1) * nblk, nblocks_total - 1), 0))
    return pl.pallas_call(
        functools.partial(_mixer_body, tiles_per_seq, nblk),
        grid=(n // tile,),
        in_specs=[row(D_MODEL), row(D_MODEL), row(D_MODEL), row(D_MODEL), row(KV_COLS), prev, nxt,
                  row(D_MODEL), row(D_MODEL),
                  const((3, N_HEADS, BLOCK, BAND)),
                  pl.BlockSpec(memory_space=pltpu.SMEM),
                  const((SGU_GROUPS, CHUNK, CHUNK)), const((SGU_GROUPS, CHUNK, CHUNK)),
                  const((D_MODEL, D_MODEL)), const((1, D_MODEL)),
                  const((D_MODEL, D_FF)), const((D_FF, D_MODEL))],
        out_specs=row(D_MODEL),
        out_shape=jax.ShapeDtypeStruct((n, D_MODEL), F32),
        scratch_shapes=[pltpu.VMEM((tile + 2 * BLOCK, KV_COLS), BF16),
                        pltpu.VMEM((tile, D_MODEL), BF16)],
        compiler_params=pltpu.CompilerParams(dimension_semantics=("arbitrary",),
                                             vmem_limit_bytes=VMEM_LIMIT_BYTES),
        name="mixer",
    )(x2d, ua, vn, q, kv, kv, kv, ga, gb, tbl, sink, ws, bs, wo, n2g, w1, w2)


def kernel(x_prompt, x_sample, rel_bias, norm1_g, w_in, sgu_norm_g, w_spatial, b_spatial,
           q_norm_g, k_norm_g, sink_logit, w_o, norm2_g, w_ff1, w_ff2):
    assert norm1_g.shape[0] == 1, "single-layer trunk"
    tbl = _bias_table(rel_bias)
    g1 = norm1_g[0].reshape(1, D_MODEL)
    sgu_g = sgu_norm_g[0].reshape(1, D_MODEL)
    q_g = q_norm_g[0].reshape(1, HEAD_DIM)
    k_g = k_norm_g[0].reshape(1, HEAD_DIM)
    n2g = norm2_g[0].reshape(1, D_MODEL)
    w_in_b = w_in[0].astype(BF16)
    ws = w_spatial[0].astype(BF16)
    bs = jnp.broadcast_to(b_spatial[0][:, :, None], (SGU_GROUPS, CHUNK, CHUNK))
    wo = w_o[0].astype(BF16)
    w1 = w_ff1[0].astype(BF16)
    w2 = w_ff2[0].astype(BF16)
    sink = sink_logit[0]

    def trunk(x):
        b, s, d = x.shape
        x2d = x.reshape(b * s, d)
        acts = _inproj(x2d, g1, w_in_b, sgu_g, q_g, k_g, tile=512)
        y = _mixer(x2d, acts, tbl, sink, ws, bs, wo, n2g, w1, w2, seq=s, tile=512)
        return y.reshape(b, s, d)

    return (trunk(x_prompt), trunk(x_sample))
```

```python
import functools
import math

import jax
import jax.numpy as jnp
from jax import lax
from jax.experimental import pallas as pl
from jax.experimental.pallas import tpu as pltpu

D_MODEL = 1024
CHUNK = 128
SGU_GROUPS = 8
N_HEADS = 8
N_KV_HEADS = 2
HEAD_DIM = 128
Q_PER_KV = N_HEADS // N_KV_HEADS
WINDOW = 128
BLOCK = 128
BAND = 3 * BLOCK
N_BUCKETS = 32
MAX_DISTANCE = 128
D_FF = 4 * D_MODEL
EPS = 1e-6
NEG_INF = -1e30
LOG2E = math.log2(math.e)
KV_DIM = N_KV_HEADS * HEAD_DIM
KV_COLS = 2 * KV_DIM
IN_COLS = 2 * D_MODEL + D_MODEL + KV_COLS + 2 * D_MODEL
PROJ_COLS = 256
PROJ_ROWS = 256
FFN_ROWS = 256

F32 = jnp.float32
BF16 = jnp.bfloat16

VMEM_LIMIT_BYTES = 56 * 1024 * 1024


def _t5_bucket(rel):
    nb = N_BUCKETS // 2
    max_exact = nb // 2
    ret = jnp.where(rel > 0, nb, 0)
    n = jnp.abs(rel)
    nf = jnp.maximum(n, 1).astype(F32)
    large = max_exact + (jnp.log(nf / max_exact) / math.log(MAX_DISTANCE / max_exact)
                         * (nb - max_exact)).astype(jnp.int32)
    large = jnp.minimum(large, nb - 1)
    return ret + jnp.where(n < max_exact, n, large)


def _rms(x, g):
    ms = jnp.mean(x * x, axis=-1, keepdims=True)
    return x * lax.rsqrt(ms + EPS) * g


def _gelu(x):
    c = math.sqrt(2.0 / math.pi)
    hx = 0.5 * x
    return hx + hx * jnp.tanh(x * (c + (c * 0.044715) * (x * x)))


def _sigmoid(x):
    return 1.0 / (1.0 + jnp.exp2(x * (-LOG2E)))


def _bias_table_body(bucket_ref, relb_ref, out_ref):
    bucket = bucket_ref[...]
    r = lax.broadcasted_iota(jnp.int32, (BAND, BLOCK), 0)
    p = lax.broadcasted_iota(jnp.int32, (BAND, BLOCK), 1)
    in_win = jnp.abs(r - BLOCK - p) <= WINDOW
    for h in range(N_HEADS):
        kh, g = divmod(h, Q_PER_KV)
        lanes = slice(g * BLOCK, (g + 1) * BLOCK)
        acc = jnp.zeros((BAND, BLOCK), F32)
        for b in range(N_BUCKETS):
            acc = jnp.where(bucket == b, relb_ref[b, h], acc)
        base = jnp.where(in_win, acc * LOG2E, NEG_INF)
        out_ref[0, kh, :, lanes] = jnp.where(r >= BLOCK, base, NEG_INF)
        out_ref[1, kh, :, lanes] = base
        out_ref[2, kh, :, lanes] = jnp.where(r < 2 * BLOCK, base, NEG_INF)


def _bias_table(rel_bias):
    r_idx = jnp.arange(BAND, dtype=jnp.int32)[:, None]
    p_idx = jnp.arange(BLOCK, dtype=jnp.int32)[None, :]
    bucket = _t5_bucket(r_idx - BLOCK - p_idx).astype(jnp.int32)
    return pl.pallas_call(
        _bias_table_body,
        out_shape=jax.ShapeDtypeStruct((3, N_KV_HEADS, BAND, Q_PER_KV * BLOCK), F32),
        in_specs=[pl.BlockSpec(memory_space=pltpu.VMEM),
                  pl.BlockSpec(memory_space=pltpu.SMEM)],
        out_specs=pl.BlockSpec(memory_space=pltpu.VMEM),
        name="bias_table",
    )(bucket, rel_bias)


def _inproj_body(x_ref, g1_ref, w_ref, wvt_ref, sgug_ref, qg_ref, kg_ref,
                 ua_ref, vn_ref, q_ref, k_ref, vt_ref, ga_ref, gb_ref, h_ref, gv_ref):
    h_ref[...] = _rms(x_ref[...], g1_ref[...]).astype(BF16)
    chunks = D_MODEL // PROJ_COLS

    def head_norm(z, g):
        return jnp.concatenate(
            [_rms(z[:, c:c + HEAD_DIM], g) for c in range(0, z.shape[1], HEAD_DIM)], axis=1)

    def project_rows(rows):
        def proj(lo):
            return jnp.dot(h_ref[rows, :], w_ref[:, lo:lo + PROJ_COLS], preferred_element_type=F32)

        ga_lo, gb_lo = 3 * D_MODEL + KV_COLS, 4 * D_MODEL + KV_COLS
        for c in range(chunks):
            cols = slice(c * PROJ_COLS, (c + 1) * PROJ_COLS)
            ua_ref[rows, cols] = _gelu(proj(c * PROJ_COLS)).astype(BF16)
            ga_ref[rows, cols] = _sigmoid(proj(ga_lo + c * PROJ_COLS)).astype(BF16)

        ssq = jnp.zeros((PROJ_ROWS, 1), F32)
        for c in range(chunks):
            cols = slice(c * PROJ_COLS, (c + 1) * PROJ_COLS)
            gv = _gelu(proj(D_MODEL + c * PROJ_COLS))
            gv_ref[rows, cols] = gv
            ssq = ssq + jnp.sum(gv * gv, axis=-1, keepdims=True)
            gb_ref[rows, cols] = _sigmoid(proj(gb_lo + c * PROJ_COLS)).astype(BF16)
        vscale = lax.rsqrt(ssq * (1.0 / D_MODEL) + EPS)
        vn_ref[rows, :] = (gv_ref[rows, :] * vscale * sgug_ref[...]).astype(BF16)

        k_ref[rows, :] = head_norm(proj(3 * D_MODEL), kg_ref[...]).astype(BF16)
        qg = qg_ref[...] * (HEAD_DIM ** -0.5 * LOG2E)
        for c in range(chunks):
            cols = slice(c * PROJ_COLS, (c + 1) * PROJ_COLS)
            q_ref[rows, cols] = head_norm(proj(2 * D_MODEL + c * PROJ_COLS), qg).astype(BF16)
        vt_ref[:, rows] = lax.dot_general(wvt_ref[...], h_ref[rows, :], (((1,), (1,)), ((), ())),
                                          preferred_element_type=F32).astype(BF16)

    for r in range(0, x_ref.shape[0], PROJ_ROWS):
        project_rows(slice(r, r + PROJ_ROWS))


def _inproj(x2d, g1, w_in, wv_t, sgu_g, q_g, k_g, tile):
    n = x2d.shape[0]
    assert n % tile == 0 and tile % PROJ_ROWS == 0
    row = lambda width: pl.BlockSpec((tile, width), lambda i: (i, 0))
    const = lambda shape: pl.BlockSpec(shape, lambda i: (0,) * len(shape), pipeline_mode=pl.Buffered(1))
    out = lambda width: jax.ShapeDtypeStruct((n, width), BF16)
    vt_spec = pl.BlockSpec((KV_DIM, tile), lambda i: (0, i))
    vt_out = jax.ShapeDtypeStruct((KV_DIM, n), BF16)
    return pl.pallas_call(
        _inproj_body,
        grid=(n // tile,),
        in_specs=[row(D_MODEL), const((1, D_MODEL)), const((D_MODEL, IN_COLS)), const((KV_DIM, D_MODEL)),
                  const((1, D_MODEL)), const((1, HEAD_DIM)), const((1, HEAD_DIM))],
        out_specs=[row(D_MODEL), row(D_MODEL), row(D_MODEL), row(KV_DIM), vt_spec, row(D_MODEL), row(D_MODEL)],
        out_shape=[out(D_MODEL), out(D_MODEL), out(D_MODEL), out(KV_DIM), vt_out, out(D_MODEL), out(D_MODEL)],
        scratch_shapes=[pltpu.VMEM((tile, D_MODEL), BF16), pltpu.VMEM((tile, D_MODEL), F32)],
        compiler_params=pltpu.CompilerParams(dimension_semantics=("arbitrary",),
                                             vmem_limit_bytes=VMEM_LIMIT_BYTES),
        name="inproj",
    )(x2d, g1, w_in, wv_t, sgu_g, q_g, k_g)


def _mixer_body(tiles_per_seq, nblk,
                x_ref, ua_ref, vn_ref, q_ref, k_ref, kp_ref, kn_ref, vt_ref, vtp_ref, vtn_ref, ga_ref, gb_ref,
                tbl_ref, sink_ref, ws_ref, bs_ref, wo_ref, n2g_ref, w1_ref, w2_ref,
                o_ref, kband_ref, vtband_ref, merged_ref):
    tile = x_ref.shape[0]
    jt = pl.program_id(0) % tiles_per_seq

    kband_ref[0:BLOCK, :] = kp_ref[...]
    kband_ref[BLOCK:BLOCK + tile, :] = k_ref[...]
    kband_ref[BLOCK + tile:, :] = kn_ref[...]
    vtband_ref[:, 0:BLOCK] = vtp_ref[...]
    vtband_ref[:, BLOCK:BLOCK + tile] = vt_ref[...]
    vtband_ref[:, BLOCK + tile:] = vtn_ref[...]

    def block(i):
        rows = slice(i * BLOCK, (i + 1) * BLOCK)
        band = slice(i * BLOCK, i * BLOCK + BAND)
        variant = 1
        if i == 0:
            variant = jnp.where(jt == 0, 0, variant)
        if i == nblk - 1:
            variant = jnp.where(jt == tiles_per_seq - 1, 2, variant)
        for kh in range(N_KV_HEADS):
            feat = slice(kh * HEAD_DIM, (kh + 1) * HEAD_DIM)
            heads = [kh * Q_PER_KV + g for g in range(Q_PER_KV)]
            qcat = jnp.concatenate(
                [q_ref[rows, hd * HEAD_DIM:(hd + 1) * HEAD_DIM] for hd in heads], axis=0)
            st = lax.dot_general(kband_ref[band, feat], qcat, (((1,), (1,)), ((), ())),
                                 preferred_element_type=F32)
            yield
            probs, inv_denoms = [], []
            for g, hd in enumerate(heads):
                lanes = slice(g * BLOCK, (g + 1) * BLOCK)
                sg = st[:, lanes] + tbl_ref[variant, kh, :, lanes]
                sink = sink_ref[hd] * LOG2E
                m = jnp.maximum(jnp.max(sg, axis=0, keepdims=True), sink)
                e = jnp.exp2(sg - m)
                inv_denoms.append(1.0 / (jnp.sum(e, axis=0, keepdims=True) + jnp.exp2(sink - m)))
                probs.append(e.astype(BF16))
                yield
            ot = jnp.dot(vtband_ref[feat, band], jnp.concatenate(probs, axis=1),
                         preferred_element_type=F32)
            yield
            for g, hd in enumerate(heads):
                cols = slice(hd * HEAD_DIM, (hd + 1) * HEAD_DIM)
                lanes = slice(g * BLOCK, (g + 1) * BLOCK)
                yb = (ot[:, lanes] * inv_denoms[g]).T.astype(BF16)
                mixed = jnp.dot(ws_ref[hd], vn_ref[rows, cols], preferred_element_type=F32) + bs_ref[hd]
                ya = ua_ref[rows, cols] * mixed.astype(BF16)
                merged_ref[rows, cols] = ga_ref[rows, cols] * ya + gb_ref[rows, cols] * yb
                yield

    def channel_mix(rows):
        x1 = x_ref[rows, :] + jnp.dot(merged_ref[rows, :], wo_ref[...], preferred_element_type=F32)
        h2 = _rms(x1, n2g_ref[...]).astype(BF16)
        yield
        acc = x1
        for c in range(D_FF // D_MODEL):
            sl = slice(c * D_MODEL, (c + 1) * D_MODEL)
            hid = jnp.dot(h2, w1_ref[:, sl], preferred_element_type=F32)
            hid = jnp.square(jnp.maximum(hid, 0.0)).astype(BF16)
            yield
            acc = acc + jnp.dot(hid, w2_ref[sl, :], preferred_element_type=F32)
            yield
        o_ref[rows, :] = acc

    def run(*gens):
        for gen in gens:
            for _ in gen:
                pass

    def interleave(main, filler, per_main):
        for _ in main:
            for _ in range(per_main):
                next(filler, None)
        run(filler)

    per_group = FFN_ROWS // BLOCK
    groups = nblk // per_group

    def mix_group(gi):
        live = [block(i) for i in range(gi * per_group, (gi + 1) * per_group)]
        while live:
            for gen in list(live):
                try:
                    next(gen)
                    yield
                except StopIteration:
                    live.remove(gen)

    def ffn_group(gi):
        return channel_mix(slice(gi * FFN_ROWS, (gi + 1) * FFN_ROWS))

    pieces_per_block = N_KV_HEADS * (2 + 2 * Q_PER_KV)
    ffn_pieces = 1 + 2 * (D_FF // D_MODEL)
    per_main = -(-per_group * pieces_per_block // ffn_pieces)
    run(mix_group(0))
    for gi in range(1, groups):
        interleave(ffn_group(gi - 1), mix_group(gi), per_main)
    run(ffn_group(groups - 1))


def _mixer(x2d, acts, tbl, sink, ws, bs, wo, n2g, w1, w2, seq, tile):
    ua, vn, q, k, vt, ga, gb = acts
    n = x2d.shape[0]
    assert seq % tile == 0 and tile % FFN_ROWS == 0 and seq >= 2 * BLOCK
    nblk = tile // BLOCK
    tiles_per_seq = seq // tile
    last_block = n // BLOCK - 1
    prev_block = lambda j: jnp.maximum(j * nblk - 1, 0)
    next_block = lambda j: jnp.minimum((j + 1) * nblk, last_block)
    row = lambda width: pl.BlockSpec((tile, width), lambda j: (j, 0))
    const = lambda shape: pl.BlockSpec(shape, lambda j: (0,) * len(shape), pipeline_mode=pl.Buffered(1))
    return pl.pallas_call(
        functools.partial(_mixer_body, tiles_per_seq, nblk),
        grid=(n // tile,),
        in_specs=[row(D_MODEL), row(D_MODEL), row(D_MODEL), row(D_MODEL),
                  row(KV_DIM),
                  pl.BlockSpec((BLOCK, KV_DIM), lambda j: (prev_block(j), 0)),
                  pl.BlockSpec((BLOCK, KV_DIM), lambda j: (next_block(j), 0)),
                  pl.BlockSpec((KV_DIM, tile), lambda j: (0, j)),
                  pl.BlockSpec((KV_DIM, BLOCK), lambda j: (0, prev_block(j))),
                  pl.BlockSpec((KV_DIM, BLOCK), lambda j: (0, next_block(j))),
                  row(D_MODEL), row(D_MODEL),
                  const((3, N_KV_HEADS, BAND, Q_PER_KV * BLOCK)),
                  pl.BlockSpec(memory_space=pltpu.SMEM),
                  const((SGU_GROUPS, CHUNK, CHUNK)), const((SGU_GROUPS, CHUNK, CHUNK)),
                  const((D_MODEL, D_MODEL)), const((1, D_MODEL)),
                  const((D_MODEL, D_FF)), const((D_FF, D_MODEL))],
        out_specs=row(D_MODEL),
        out_shape=jax.ShapeDtypeStruct((n, D_MODEL), F32),
        scratch_shapes=[pltpu.VMEM((tile + 2 * BLOCK, KV_DIM), BF16),
                        pltpu.VMEM((KV_DIM, tile + 2 * BLOCK), BF16),
                        pltpu.VMEM((tile, D_MODEL), BF16)],
        compiler_params=pltpu.CompilerParams(dimension_semantics=("arbitrary",),
                                             vmem_limit_bytes=VMEM_LIMIT_BYTES),
        name="mixer",
    )(x2d, ua, vn, q, k, k, k, vt, vt, vt, ga, gb, tbl, sink, ws, bs, wo, n2g, w1, w2)


def kernel(x_prompt, x_sample, rel_bias, norm1_g, w_in, sgu_norm_g, w_spatial, b_spatial,
           q_norm_g, k_norm_g, sink_logit, w_o, norm2_g, w_ff1, w_ff2):
    assert norm1_g.shape[0] == 1, "single-layer trunk"
    tbl = _bias_table(rel_bias)
    g1 = norm1_g[0].reshape(1, D_MODEL)
    sgu_g = sgu_norm_g[0].reshape(1, D_MODEL)
    q_g = q_norm_g[0].reshape(1, HEAD_DIM)
    k_g = k_norm_g[0].reshape(1, HEAD_DIM)
    n2g = norm2_g[0].reshape(1, D_MODEL)
    w_in_b = w_in[0].astype(BF16)
    v_lo = 3 * D_MODEL + KV_DIM
    wv_t = w_in[0][:, v_lo:v_lo + KV_DIM].T.astype(BF16)
    ws = w_spatial[0].astype(BF16)
    bs = jnp.broadcast_to(b_spatial[0][:, :, None], (SGU_GROUPS, CHUNK, CHUNK))
    wo = w_o[0].astype(BF16)
    w1 = w_ff1[0].astype(BF16)
    w2 = w_ff2[0].astype(BF16)
    sink = sink_logit[0]

    def trunk(x):
        b, s, d = x.shape
        x2d = x.reshape(b * s, d)
        acts = _inproj(x2d, g1, w_in_b, wv_t, sgu_g, q_g, k_g, tile=512)
        y = _mixer(x2d, acts, tbl, sink, ws, bs, wo, n2g, w1, w2, seq=s, tile=512)
        return y.reshape(b, s, d)

    return (trunk(x_prompt), trunk(x_sample))
```

```python
import functools
import math

import jax
import jax.numpy as jnp
from jax import lax
from jax.experimental import pallas as pl
from jax.experimental.pallas import tpu as pltpu

D_MODEL = 1024
CHUNK = 128
SGU_GROUPS = 8
N_HEADS = 8
N_KV_HEADS = 2
HEAD_DIM = 128
Q_PER_KV = N_HEADS // N_KV_HEADS
WINDOW = 128
BLOCK = 128
BAND = 3 * BLOCK
N_BUCKETS = 32
MAX_DISTANCE = 128
D_FF = 4 * D_MODEL
EPS = 1e-6
NEG_INF = -1e30
LOG2E = math.log2(math.e)
KV_DIM = N_KV_HEADS * HEAD_DIM
KV_COLS = 2 * KV_DIM
IN_COLS = 2 * D_MODEL + D_MODEL + KV_COLS + 2 * D_MODEL
PROJ_COLS = 256
PROJ_ROWS = 256
GROUP_BLOCKS = (2, 2)

F32 = jnp.float32
BF16 = jnp.bfloat16

VMEM_LIMIT_BYTES = 56 * 1024 * 1024


def _t5_bucket(rel):
    nb = N_BUCKETS // 2
    max_exact = nb // 2
    ret = jnp.where(rel > 0, nb, 0)
    n = jnp.abs(rel)
    nf = jnp.maximum(n, 1).astype(F32)
    large = max_exact + (jnp.log(nf / max_exact) / math.log(MAX_DISTANCE / max_exact)
                         * (nb - max_exact)).astype(jnp.int32)
    large = jnp.minimum(large, nb - 1)
    return ret + jnp.where(n < max_exact, n, large)


def _rms(x, g):
    ms = jnp.mean(x * x, axis=-1, keepdims=True)
    return x * lax.rsqrt(ms + EPS) * g


def _gelu(x):
    c = math.sqrt(2.0 / math.pi)
    hx = 0.5 * x
    return hx + hx * jnp.tanh(x * (c + (c * 0.044715) * (x * x)))


def _sigmoid(x):
    return 1.0 / (1.0 + jnp.exp2(x * (-LOG2E)))


def _bias_table_body(bucket_ref, relb_ref, out_ref):
    bucket = bucket_ref[...]
    r = lax.broadcasted_iota(jnp.int32, (BAND, BLOCK), 0)
    p = lax.broadcasted_iota(jnp.int32, (BAND, BLOCK), 1)
    in_win = jnp.abs(r - BLOCK - p) <= WINDOW
    for h in range(N_HEADS):
        kh, g = divmod(h, Q_PER_KV)
        lanes = slice(g * BLOCK, (g + 1) * BLOCK)
        acc = jnp.zeros((BAND, BLOCK), F32)
        for b in range(N_BUCKETS):
            acc = jnp.where(bucket == b, relb_ref[b, h], acc)
        base = jnp.where(in_win, acc * LOG2E, NEG_INF)
        out_ref[0, kh, :, lanes] = jnp.where(r >= BLOCK, base, NEG_INF)
        out_ref[1, kh, :, lanes] = base
        out_ref[2, kh, :, lanes] = jnp.where(r < 2 * BLOCK, base, NEG_INF)


def _bias_table(rel_bias):
    r_idx = jnp.arange(BAND, dtype=jnp.int32)[:, None]
    p_idx = jnp.arange(BLOCK, dtype=jnp.int32)[None, :]
    bucket = _t5_bucket(r_idx - BLOCK - p_idx).astype(jnp.int32)
    return pl.pallas_call(
        _bias_table_body,
        out_shape=jax.ShapeDtypeStruct((3, N_KV_HEADS, BAND, Q_PER_KV * BLOCK), F32),
        in_specs=[pl.BlockSpec(memory_space=pltpu.VMEM),
                  pl.BlockSpec(memory_space=pltpu.SMEM)],
        out_specs=pl.BlockSpec(memory_space=pltpu.VMEM),
        name="bias_table",
    )(bucket, rel_bias)


def _inproj_body(x_ref, g1_ref, w_ref, wvt_ref, sgug_ref, qg_ref, kg_ref, ws_ref, bs_ref,
                 gya_ref, q_ref, k_ref, vt_ref, gb_ref, h_ref, gv_ref, gu_ref, vn_ref):
    h_ref[...] = _rms(x_ref[...], g1_ref[...]).astype(BF16)
    chunks = D_MODEL // PROJ_COLS

    def head_norm(z, g):
        return jnp.concatenate(
            [_rms(z[:, c:c + HEAD_DIM], g) for c in range(0, z.shape[1], HEAD_DIM)], axis=1)

    def project_rows(rows):
        def proj(lo):
            return jnp.dot(h_ref[rows, :], w_ref[:, lo:lo + PROJ_COLS], preferred_element_type=F32)

        ga_lo, gb_lo = 3 * D_MODEL + KV_COLS, 4 * D_MODEL + KV_COLS
        for c in range(chunks):
            cols = slice(c * PROJ_COLS, (c + 1) * PROJ_COLS)
            gated_u = _gelu(proj(c * PROJ_COLS)) * _sigmoid(proj(ga_lo + c * PROJ_COLS))
            gu_ref[rows, cols] = gated_u.astype(BF16)

        ssq = jnp.zeros((PROJ_ROWS, 1), F32)
        for c in range(chunks):
            cols = slice(c * PROJ_COLS, (c + 1) * PROJ_COLS)
            gv = _gelu(proj(D_MODEL + c * PROJ_COLS))
            gv_ref[rows, cols] = gv
            ssq = ssq + jnp.sum(gv * gv, axis=-1, keepdims=True)
            gb_ref[rows, cols] = _sigmoid(proj(gb_lo + c * PROJ_COLS)).astype(BF16)
        vscale = lax.rsqrt(ssq * (1.0 / D_MODEL) + EPS)
        vn_ref[rows, :] = (gv_ref[rows, :] * vscale * sgug_ref[...]).astype(BF16)

        chunk_rows = [slice(rows.start + t, rows.start + t + CHUNK) for t in range(0, PROJ_ROWS, CHUNK)]

        def sgu_group(g):
            gcols = slice(g * CHUNK, (g + 1) * CHUNK)
            v_wide = jnp.concatenate([vn_ref[cr, gcols] for cr in chunk_rows], axis=1)
            mixed = jnp.dot(ws_ref[g], v_wide, preferred_element_type=F32)
            for t, cr in enumerate(chunk_rows):
                ya = mixed[:, t * CHUNK:(t + 1) * CHUNK] + bs_ref[g]
                gya_ref[cr, gcols] = (gu_ref[cr, gcols].astype(F32) * ya).astype(BF16)

        k_ref[rows, :] = head_norm(proj(3 * D_MODEL), kg_ref[...]).astype(BF16)
        qg = qg_ref[...] * (HEAD_DIM ** -0.5 * LOG2E)
        groups_per_chunk = SGU_GROUPS // chunks
        for c in range(chunks):
            cols = slice(c * PROJ_COLS, (c + 1) * PROJ_COLS)
            q_ref[rows, cols] = head_norm(proj(2 * D_MODEL + c * PROJ_COLS), qg).astype(BF16)
            for g in range(c * groups_per_chunk, (c + 1) * groups_per_chunk):
                sgu_group(g)
        vt_ref[:, rows] = lax.dot_general(wvt_ref[...], h_ref[rows, :], (((1,), (1,)), ((), ())),
                                          preferred_element_type=F32).astype(BF16)

    for r in range(0, x_ref.shape[0], PROJ_ROWS):
        project_rows(slice(r, r + PROJ_ROWS))


def _inproj(x2d, g1, w_in, wv_t, sgu_g, q_g, k_g, ws, bs, tile):
    n = x2d.shape[0]
    assert n % tile == 0 and tile % PROJ_ROWS == 0
    row = lambda width: pl.BlockSpec((tile, width), lambda i: (i, 0))
    const = lambda shape: pl.BlockSpec(shape, lambda i: (0,) * len(shape), pipeline_mode=pl.Buffered(1))
    out = lambda width: jax.ShapeDtypeStruct((n, width), BF16)
    vt_spec = pl.BlockSpec((KV_DIM, tile), lambda i: (0, i))
    vt_out = jax.ShapeDtypeStruct((KV_DIM, n), BF16)
    return pl.pallas_call(
        _inproj_body,
        grid=(n // tile,),
        in_specs=[row(D_MODEL), const((1, D_MODEL)), const((D_MODEL, IN_COLS)), const((KV_DIM, D_MODEL)),
                  const((1, D_MODEL)), const((1, HEAD_DIM)), const((1, HEAD_DIM)),
                  const((SGU_GROUPS, CHUNK, CHUNK)), const((SGU_GROUPS, CHUNK, CHUNK))],
        out_specs=[row(D_MODEL), row(D_MODEL), row(KV_DIM), vt_spec, row(D_MODEL)],
        out_shape=[out(D_MODEL), out(D_MODEL), out(KV_DIM), vt_out, out(D_MODEL)],
        scratch_shapes=[pltpu.VMEM((tile, D_MODEL), BF16), pltpu.VMEM((tile, D_MODEL), F32),
                        pltpu.VMEM((tile, D_MODEL), BF16), pltpu.VMEM((tile, D_MODEL), BF16)],
        compiler_params=pltpu.CompilerParams(dimension_semantics=("arbitrary",),
                                             vmem_limit_bytes=VMEM_LIMIT_BYTES),
        name="inproj",
    )(x2d, g1, w_in, wv_t, sgu_g, q_g, k_g, ws, bs)


def _mixer_body(tiles_per_seq, nblk,
                x_ref, gya_ref, q_ref, k_ref, kp_ref, kn_ref, vt_ref, vtp_ref, vtn_ref, gb_ref,
                tbl_ref, sink_ref, wo_ref, n2g_ref, w1_ref, w2_ref,
                o_ref, kband_ref, vtband_ref, merged_ref):
    tile = x_ref.shape[0]
    jt = pl.program_id(0) % tiles_per_seq

    kband_ref[0:BLOCK, :] = kp_ref[...]
    kband_ref[BLOCK:BLOCK + tile, :] = k_ref[...]
    kband_ref[BLOCK + tile:, :] = kn_ref[...]
    vtband_ref[:, 0:BLOCK] = vtp_ref[...]
    vtband_ref[:, BLOCK:BLOCK + tile] = vt_ref[...]
    vtband_ref[:, BLOCK + tile:] = vtn_ref[...]

    def block(i):
        rows = slice(i * BLOCK, (i + 1) * BLOCK)
        band = slice(i * BLOCK, i * BLOCK + BAND)
        variant = 1
        if i == 0:
            variant = jnp.where(jt == 0, 0, variant)
        if i == nblk - 1:
            variant = jnp.where(jt == tiles_per_seq - 1, 2, variant)
        for kh in range(N_KV_HEADS):
            feat = slice(kh * HEAD_DIM, (kh + 1) * HEAD_DIM)
            heads = [kh * Q_PER_KV + g for g in range(Q_PER_KV)]
            qcat = jnp.concatenate(
                [q_ref[rows, hd * HEAD_DIM:(hd + 1) * HEAD_DIM] for hd in heads], axis=0)
            st = lax.dot_general(kband_ref[band, feat], qcat, (((1,), (1,)), ((), ())),
                                 preferred_element_type=F32)
            yield
            probs, inv_denoms = [], []
            for g, hd in enumerate(heads):
                lanes = slice(g * BLOCK, (g + 1) * BLOCK)
                sg = st[:, lanes] + tbl_ref[variant, kh, :, lanes]
                sink = sink_ref[hd] * LOG2E
                m = jnp.maximum(jnp.max(sg, axis=0, keepdims=True), sink)
                e = jnp.exp2(sg - m)
                inv_denoms.append(1.0 / (jnp.sum(e, axis=0, keepdims=True) + jnp.exp2(sink - m)))
                probs.append(e.astype(BF16))
                yield
            ot = jnp.dot(vtband_ref[feat, band], jnp.concatenate(probs, axis=1),
                         preferred_element_type=F32)
            yield
            for g, hd in enumerate(heads):
                cols = slice(hd * HEAD_DIM, (hd + 1) * HEAD_DIM)
                lanes = slice(g * BLOCK, (g + 1) * BLOCK)
                yb = (ot[:, lanes] * inv_denoms[g]).T.astype(BF16)
                merged_ref[rows, cols] = gya_ref[rows, cols] + gb_ref[rows, cols] * yb
                yield

    def channel_mix(rows):
        x1 = x_ref[rows, :] + jnp.dot(merged_ref[rows, :], wo_ref[...], preferred_element_type=F32)
        h2 = _rms(x1, n2g_ref[...]).astype(BF16)
        yield
        acc = x1
        for c in range(D_FF // D_MODEL):
            sl = slice(c * D_MODEL, (c + 1) * D_MODEL)
            hid = jnp.dot(h2, w1_ref[:, sl], preferred_element_type=F32)
            hid = jnp.square(jnp.maximum(hid, 0.0)).astype(BF16)
            yield
            acc = acc + jnp.dot(hid, w2_ref[sl, :], preferred_element_type=F32)
            yield
        o_ref[rows, :] = acc

    def run(*gens):
        for gen in gens:
            for _ in gen:
                pass

    def interleave(main, filler, per_main):
        for _ in main:
            for _ in range(per_main):
                next(filler, None)
        run(filler)

    assert sum(GROUP_BLOCKS) == nblk
    starts = [sum(GROUP_BLOCKS[:gi]) for gi in range(len(GROUP_BLOCKS))]

    def mix_group(gi):
        live = [block(i) for i in range(starts[gi], starts[gi] + GROUP_BLOCKS[gi])]
        while live:
            for gen in list(live):
                try:
                    next(gen)
                    yield
                except StopIteration:
                    live.remove(gen)

    def ffn_group(gi):
        return channel_mix(slice(starts[gi] * BLOCK, (starts[gi] + GROUP_BLOCKS[gi]) * BLOCK))

    pieces_per_block = N_KV_HEADS * (2 + 2 * Q_PER_KV)
    ffn_pieces = 1 + 2 * (D_FF // D_MODEL)
    run(mix_group(0))
    for gi in range(1, len(GROUP_BLOCKS)):
        interleave(ffn_group(gi - 1), mix_group(gi), -(-GROUP_BLOCKS[gi] * pieces_per_block // ffn_pieces))
    run(ffn_group(len(GROUP_BLOCKS) - 1))


def _mixer(x2d, acts, tbl, sink, wo, n2g, w1, w2, seq, tile):
    gya, q, k, vt, gb = acts
    n = x2d.shape[0]
    assert seq % tile == 0 and tile == sum(GROUP_BLOCKS) * BLOCK and seq >= 2 * BLOCK
    nblk = tile // BLOCK
    tiles_per_seq = seq // tile
    last_block = n // BLOCK - 1
    prev_block = lambda j: jnp.maximum(j * nblk - 1, 0)
    next_block = lambda j: jnp.minimum((j + 1) * nblk, last_block)
    row = lambda width: pl.BlockSpec((tile, width), lambda j: (j, 0))
    const = lambda shape: pl.BlockSpec(shape, lambda j: (0,) * len(shape), pipeline_mode=pl.Buffered(1))
    return pl.pallas_call(
        functools.partial(_mixer_body, tiles_per_seq, nblk),
        grid=(n // tile,),
        in_specs=[row(D_MODEL), row(D_MODEL), row(D_MODEL),
                  row(KV_DIM),
                  pl.BlockSpec((BLOCK, KV_DIM), lambda j: (prev_block(j), 0)),
                  pl.BlockSpec((BLOCK, KV_DIM), lambda j: (next_block(j), 0)),
                  pl.BlockSpec((KV_DIM, tile), lambda j: (0, j)),
                  pl.BlockSpec((KV_DIM, BLOCK), lambda j: (0, prev_block(j))),
                  pl.BlockSpec((KV_DIM, BLOCK), lambda j: (0, next_block(j))),
                  row(D_MODEL),
                  const((3, N_KV_HEADS, BAND, Q_PER_KV * BLOCK)),
                  pl.BlockSpec(memory_space=pltpu.SMEM),
                  const((D_MODEL, D_MODEL)), const((1, D_MODEL)),
                  const((D_MODEL, D_FF)), const((D_FF, D_MODEL))],
        out_specs=row(D_MODEL),
        out_shape=jax.ShapeDtypeStruct((n, D_MODEL), F32),
        scratch_shapes=[pltpu.VMEM((tile + 2 * BLOCK, KV_DIM), BF16),
                        pltpu.VMEM((KV_DIM, tile + 2 * BLOCK), BF16),
                        pltpu.VMEM((tile, D_MODEL), BF16)],
        compiler_params=pltpu.CompilerParams(dimension_semantics=("arbitrary",),
                                             vmem_limit_bytes=VMEM_LIMIT_BYTES),
        name="mixer",
    )(x2d, gya, q, k, k, k, vt, vt, vt, gb, tbl, sink, wo, n2g, w1, w2)


def kernel(x_prompt, x_sample, rel_bias, norm1_g, w_in, sgu_norm_g, w_spatial, b_spatial,
           q_norm_g, k_norm_g, sink_logit, w_o, norm2_g, w_ff1, w_ff2):
    assert norm1_g.shape[0] == 1, "single-layer trunk"
    tbl = _bias_table(rel_bias)
    g1 = norm1_g[0].reshape(1, D_MODEL)
    sgu_g = sgu_norm_g[0].reshape(1, D_MODEL)
    q_g = q_norm_g[0].reshape(1, HEAD_DIM)
    k_g = k_norm_g[0].reshape(1, HEAD_DIM)
    n2g = norm2_g[0].reshape(1, D_MODEL)
    w_in_b = w_in[0].astype(BF16)
    v_lo = 3 * D_MODEL + KV_DIM
    wv_t = w_in_b[:, v_lo:v_lo + KV_DIM].T
    ws = w_spatial[0].astype(BF16)
    bs = jnp.broadcast_to(b_spatial[0][:, :, None], (SGU_GROUPS, CHUNK, CHUNK))
    wo = w_o[0].astype(BF16)
    w1 = w_ff1[0].astype(BF16)
    w2 = w_ff2[0].astype(BF16)
    sink = sink_logit[0]

    def trunk(x):
        b, s, d = x.shape
        x2d = x.reshape(b * s, d)
        acts = _inproj(x2d, g1, w_in_b, wv_t, sgu_g, q_g, k_g, ws, bs, tile=1024)
        y = _mixer(x2d, acts, tbl, sink, wo, n2g, w1, w2, seq=s, tile=512)
        return y.reshape(b, s, d)

    return (trunk(x_prompt), trunk(x_sample))
```

```python
import functools
import math

import jax
import jax.numpy as jnp
from jax import lax
from jax.experimental import pallas as pl
from jax.experimental.pallas import tpu as pltpu

D_MODEL = 1024
CHUNK = 128
SGU_GROUPS = 8
N_HEADS = 8
N_KV_HEADS = 2
HEAD_DIM = 128
Q_PER_KV = N_HEADS // N_KV_HEADS
WINDOW = 128
BLOCK = 128
BAND = 3 * BLOCK
N_BUCKETS = 32
MAX_DISTANCE = 128
D_FF = 4 * D_MODEL
EPS = 1e-6
NEG_INF = -1e30
LOG2E = math.log2(math.e)
KV_DIM = N_KV_HEADS * HEAD_DIM
KV_COLS = 2 * KV_DIM
IN_COLS = 2 * D_MODEL + D_MODEL + KV_COLS + 2 * D_MODEL
PROJ_COLS = 256
PROJ_ROWS = 256
GROUP_BLOCKS = (2, 2)

F32 = jnp.float32
BF16 = jnp.bfloat16

VMEM_LIMIT_BYTES = 56 * 1024 * 1024


def _t5_bucket(rel):
    nb = N_BUCKETS // 2
    max_exact = nb // 2
    ret = jnp.where(rel > 0, nb, 0)
    n = jnp.abs(rel)
    nf = jnp.maximum(n, 1).astype(F32)
    large = max_exact + (jnp.log(nf / max_exact) / math.log(MAX_DISTANCE / max_exact)
                         * (nb - max_exact)).astype(jnp.int32)
    large = jnp.minimum(large, nb - 1)
    return ret + jnp.where(n < max_exact, n, large)


def _rms(x, g):
    ms = jnp.mean(x * x, axis=-1, keepdims=True)
    return x * lax.rsqrt(ms + EPS) * g


def _gelu(x):
    c = math.sqrt(2.0 / math.pi)
    hx = 0.5 * x
    return hx + hx * jnp.tanh(x * (c + (c * 0.044715) * (x * x)))


def _sigmoid(x):
    return 1.0 / (1.0 + jnp.exp2(x * (-LOG2E)))


def _bias_table_body(bucket_ref, relb_ref, out_ref):
    bucket = bucket_ref[...]
    r = lax.broadcasted_iota(jnp.int32, (BAND, BLOCK), 0)
    p = lax.broadcasted_iota(jnp.int32, (BAND, BLOCK), 1)
    in_win = jnp.abs(r - BLOCK - p) <= WINDOW
    for h in range(N_HEADS):
        kh, g = divmod(h, Q_PER_KV)
        lanes = slice(g * BLOCK, (g + 1) * BLOCK)
        acc = jnp.zeros((BAND, BLOCK), F32)
        for b in range(N_BUCKETS):
            acc = jnp.where(bucket == b, relb_ref[b, h], acc)
        base = jnp.where(in_win, acc * LOG2E, NEG_INF)
        out_ref[0, kh, :, lanes] = jnp.where(r >= BLOCK, base, NEG_INF)
        out_ref[1, kh, :, lanes] = base
        out_ref[2, kh, :, lanes] = jnp.where(r < 2 * BLOCK, base, NEG_INF)


def _bias_table(rel_bias):
    r_idx = jnp.arange(BAND, dtype=jnp.int32)[:, None]
    p_idx = jnp.arange(BLOCK, dtype=jnp.int32)[None, :]
    bucket = _t5_bucket(r_idx - BLOCK - p_idx).astype(jnp.int32)
    return pl.pallas_call(
        _bias_table_body,
        out_shape=jax.ShapeDtypeStruct((3, N_KV_HEADS, BAND, Q_PER_KV * BLOCK), F32),
        in_specs=[pl.BlockSpec(memory_space=pltpu.VMEM),
                  pl.BlockSpec(memory_space=pltpu.SMEM)],
        out_specs=pl.BlockSpec(memory_space=pltpu.VMEM),
        name="bias_table",
    )(bucket, rel_bias)


def _inproj_body(x_ref, g1_ref, w_ref, wv_ref, sgug_ref, qg_ref, kg_ref, ws_ref, bs_ref,
                 gya_ref, q_ref, k_ref, vt_ref, gb_ref, h_ref, gv_ref, gu_ref, vn_ref, wvt_ref):
    @pl.when(pl.program_id(0) == 0)
    def _():
        wvt_ref[...] = wv_ref[0].T.astype(BF16)

    h_ref[...] = _rms(x_ref[...], g1_ref[...]).astype(BF16)
    chunks = D_MODEL // PROJ_COLS

    def head_norm(z, g):
        return jnp.concatenate(
            [_rms(z[:, c:c + HEAD_DIM], g) for c in range(0, z.shape[1], HEAD_DIM)], axis=1)

    def project_rows(rows):
        def proj(lo):
            return jnp.dot(h_ref[rows, :], w_ref[:, lo:lo + PROJ_COLS], preferred_element_type=F32)

        ga_lo, gb_lo = 3 * D_MODEL + KV_COLS, 4 * D_MODEL + KV_COLS
        for c in range(chunks):
            cols = slice(c * PROJ_COLS, (c + 1) * PROJ_COLS)
            gated_u = _gelu(proj(c * PROJ_COLS)) * _sigmoid(proj(ga_lo + c * PROJ_COLS))
            gu_ref[rows, cols] = gated_u.astype(BF16)

        ssq = jnp.zeros((PROJ_ROWS, 1), F32)
        for c in range(chunks):
            cols = slice(c * PROJ_COLS, (c + 1) * PROJ_COLS)
            gv = _gelu(proj(D_MODEL + c * PROJ_COLS))
            gv_ref[rows, cols] = gv
            ssq = ssq + jnp.sum(gv * gv, axis=-1, keepdims=True)
            gb_ref[rows, cols] = _sigmoid(proj(gb_lo + c * PROJ_COLS)).astype(BF16)
        vscale = lax.rsqrt(ssq * (1.0 / D_MODEL) + EPS)
        vn_ref[rows, :] = (gv_ref[rows, :] * vscale * sgug_ref[...]).astype(BF16)

        chunk_rows = [slice(rows.start + t, rows.start + t + CHUNK) for t in range(0, PROJ_ROWS, CHUNK)]

        def sgu_group(g):
            gcols = slice(g * CHUNK, (g + 1) * CHUNK)
            v_wide = jnp.concatenate([vn_ref[cr, gcols] for cr in chunk_rows], axis=1)
            mixed = jnp.dot(ws_ref[g], v_wide, preferred_element_type=F32)
            for t, cr in enumerate(chunk_rows):
                ya = mixed[:, t * CHUNK:(t + 1) * CHUNK] + bs_ref[g]
                gya_ref[cr, gcols] = (gu_ref[cr, gcols].astype(F32) * ya).astype(BF16)

        k_ref[rows, :] = head_norm(proj(3 * D_MODEL), kg_ref[...]).astype(BF16)
        qg = qg_ref[...] * (HEAD_DIM ** -0.5 * LOG2E)
        groups_per_chunk = SGU_GROUPS // chunks
        for c in range(chunks):
            cols = slice(c * PROJ_COLS, (c + 1) * PROJ_COLS)
            q_ref[rows, cols] = head_norm(proj(2 * D_MODEL + c * PROJ_COLS), qg).astype(BF16)
            for g in range(c * groups_per_chunk, (c + 1) * groups_per_chunk):
                sgu_group(g)
        vt_ref[:, rows] = lax.dot_general(wvt_ref[...], h_ref[rows, :], (((1,), (1,)), ((), ())),
                                          preferred_element_type=F32).astype(BF16)

    for r in range(0, x_ref.shape[0], PROJ_ROWS):
        project_rows(slice(r, r + PROJ_ROWS))


def _inproj(x2d, g1, w_in, w_in_f32, sgu_g, q_g, k_g, ws, bs, tile):
    n = x2d.shape[0]
    assert n % tile == 0 and tile % PROJ_ROWS == 0
    row = lambda width: pl.BlockSpec((tile, width), lambda i: (i, 0))
    const = lambda shape: pl.BlockSpec(shape, lambda i: (0,) * len(shape), pipeline_mode=pl.Buffered(1))
    out = lambda width: jax.ShapeDtypeStruct((n, width), BF16)
    vt_spec = pl.BlockSpec((KV_DIM, tile), lambda i: (0, i))
    vt_out = jax.ShapeDtypeStruct((KV_DIM, n), BF16)
    v_block = (3 * D_MODEL + KV_DIM) // KV_DIM
    wv_spec = pl.BlockSpec((1, D_MODEL, KV_DIM), lambda i: (0, 0, v_block), pipeline_mode=pl.Buffered(1))
    return pl.pallas_call(
        _inproj_body,
        grid=(n // tile,),
        in_specs=[row(D_MODEL), const((1, D_MODEL)), const((D_MODEL, IN_COLS)), wv_spec,
                  const((1, D_MODEL)), const((1, HEAD_DIM)), const((1, HEAD_DIM)),
                  const((SGU_GROUPS, CHUNK, CHUNK)), const((SGU_GROUPS, CHUNK, CHUNK))],
        out_specs=[row(D_MODEL), row(D_MODEL), row(KV_DIM), vt_spec, row(D_MODEL)],
        out_shape=[out(D_MODEL), out(D_MODEL), out(KV_DIM), vt_out, out(D_MODEL)],
        scratch_shapes=[pltpu.VMEM((tile, D_MODEL), BF16), pltpu.VMEM((tile, D_MODEL), F32),
                        pltpu.VMEM((tile, D_MODEL), BF16), pltpu.VMEM((tile, D_MODEL), BF16),
                        pltpu.VMEM((KV_DIM, D_MODEL), BF16)],
        compiler_params=pltpu.CompilerParams(dimension_semantics=("arbitrary",),
                                             vmem_limit_bytes=VMEM_LIMIT_BYTES),
        name="inproj",
    )(x2d, g1, w_in, w_in_f32, sgu_g, q_g, k_g, ws, bs)


def _mixer_body(tiles_per_seq, nblk,
                x_ref, gya_ref, q_ref, k_ref, kp_ref, kn_ref, vt_ref, vtp_ref, vtn_ref, gb_ref,
                tbl_ref, sink_ref, wo_ref, n2g_ref, w1_ref, w2_ref,
                o_ref, kband_ref, vtband_ref, merged_ref):
    tile = x_ref.shape[0]
    jt = pl.program_id(0) % tiles_per_seq

    kband_ref[0:BLOCK, :] = kp_ref[...]
    kband_ref[BLOCK:BLOCK + tile, :] = k_ref[...]
    kband_ref[BLOCK + tile:, :] = kn_ref[...]
    vtband_ref[:, 0:BLOCK] = vtp_ref[...]
    vtband_ref[:, BLOCK:BLOCK + tile] = vt_ref[...]
    vtband_ref[:, BLOCK + tile:] = vtn_ref[...]

    def block(i, kh):
        rows = slice(i * BLOCK, (i + 1) * BLOCK)
        band = slice(i * BLOCK, i * BLOCK + BAND)
        variant = 1
        if i == 0:
            variant = jnp.where(jt == 0, 0, variant)
        if i == nblk - 1:
            variant = jnp.where(jt == tiles_per_seq - 1, 2, variant)
        feat = slice(kh * HEAD_DIM, (kh + 1) * HEAD_DIM)
        heads = [kh * Q_PER_KV + g for g in range(Q_PER_KV)]
        qcat = jnp.concatenate(
            [q_ref[rows, hd * HEAD_DIM:(hd + 1) * HEAD_DIM] for hd in heads], axis=0)
        st = lax.dot_general(kband_ref[band, feat], qcat, (((1,), (1,)), ((), ())),
                             preferred_element_type=F32)
        yield
        probs, inv_denoms = [], []
        for g, hd in enumerate(heads):
            lanes = slice(g * BLOCK, (g + 1) * BLOCK)
            sg = st[:, lanes] + tbl_ref[variant, kh, :, lanes]
            sink = sink_ref[hd] * LOG2E
            m = jnp.maximum(jnp.max(sg, axis=0, keepdims=True), sink)
            e = jnp.exp2(sg - m)
            inv_denoms.append(1.0 / (jnp.sum(e, axis=0, keepdims=True) + jnp.exp2(sink - m)))
            probs.append(e.astype(BF16))
            yield
        ot = jnp.dot(vtband_ref[feat, band], jnp.concatenate(probs, axis=1),
                     preferred_element_type=F32)
        yield
        for g, hd in enumerate(heads):
            cols = slice(hd * HEAD_DIM, (hd + 1) * HEAD_DIM)
            lanes = slice(g * BLOCK, (g + 1) * BLOCK)
            yb = (ot[:, lanes] * inv_denoms[g]).T.astype(BF16)
            merged_ref[rows, cols] = gya_ref[rows, cols] + gb_ref[rows, cols] * yb
            yield

    def channel_mix(rows):
        x1 = x_ref[rows, :] + jnp.dot(merged_ref[rows, :], wo_ref[...], preferred_element_type=F32)
        h2 = _rms(x1, n2g_ref[...]).astype(BF16)
        yield
        acc = x1
        for c in range(D_FF // D_MODEL):
            sl = slice(c * D_MODEL, (c + 1) * D_MODEL)
            hid = jnp.dot(h2, w1_ref[:, sl], preferred_element_type=F32)
            hid = jnp.square(jnp.maximum(hid, 0.0)).astype(BF16)
            yield
            acc = acc + jnp.dot(hid, w2_ref[sl, :], preferred_element_type=F32)
            yield
        o_ref[rows, :] = acc

    def run(*gens):
        for gen in gens:
            for _ in gen:
                pass

    def interleave(main, filler, per_main):
        for _ in main:
            for _ in range(per_main):
                next(filler, None)
        run(filler)

    assert sum(GROUP_BLOCKS) == nblk
    starts = [sum(GROUP_BLOCKS[:gi]) for gi in range(len(GROUP_BLOCKS))]

    def mix_group(gi):
        live = [block(i, kh) for i in range(starts[gi], starts[gi] + GROUP_BLOCKS[gi])
                for kh in range(N_KV_HEADS)]
        while live:
            for gen in list(live):
                try:
                    next(gen)
                    yield
                except StopIteration:
                    live.remove(gen)

    def ffn_group(gi):
        return channel_mix(slice(starts[gi] * BLOCK, (starts[gi] + GROUP_BLOCKS[gi]) * BLOCK))

    pieces_per_block = N_KV_HEADS * (2 + 2 * Q_PER_KV)
    ffn_pieces = 1 + 2 * (D_FF // D_MODEL)
    run(mix_group(0))
    for gi in range(1, len(GROUP_BLOCKS)):
        interleave(ffn_group(gi - 1), mix_group(gi), -(-GROUP_BLOCKS[gi] * pieces_per_block // ffn_pieces))
    run(ffn_group(len(GROUP_BLOCKS) - 1))


def _mixer(x2d, acts, tbl, sink, wo, n2g, w1, w2, seq, tile):
    gya, q, k, vt, gb = acts
    n = x2d.shape[0]
    assert seq % tile == 0 and tile == sum(GROUP_BLOCKS) * BLOCK and seq >= 2 * BLOCK
    nblk = tile // BLOCK
    tiles_per_seq = seq // tile
    last_block = n // BLOCK - 1
    prev_block = lambda j: jnp.maximum(j * nblk - 1, 0)
    next_block = lambda j: jnp.minimum((j + 1) * nblk, last_block)
    row = lambda width: pl.BlockSpec((tile, width), lambda j: (j, 0))
    const = lambda shape: pl.BlockSpec(shape, lambda j: (0,) * len(shape), pipeline_mode=pl.Buffered(1))
    return pl.pallas_call(
        functools.partial(_mixer_body, tiles_per_seq, nblk),
        grid=(n // tile,),
        in_specs=[row(D_MODEL), row(D_MODEL), row(D_MODEL),
                  row(KV_DIM),
                  pl.BlockSpec((BLOCK, KV_DIM), lambda j: (prev_block(j), 0)),
                  pl.BlockSpec((BLOCK, KV_DIM), lambda j: (next_block(j), 0)),
                  pl.BlockSpec((KV_DIM, tile), lambda j: (0, j)),
                  pl.BlockSpec((KV_DIM, BLOCK), lambda j: (0, prev_block(j))),
                  pl.BlockSpec((KV_DIM, BLOCK), lambda j: (0, next_block(j))),
                  row(D_MODEL),
                  const((3, N_KV_HEADS, BAND, Q_PER_KV * BLOCK)),
                  pl.BlockSpec(memory_space=pltpu.SMEM),
                  const((D_MODEL, D_MODEL)), const((1, D_MODEL)),
                  const((D_MODEL, D_FF)), const((D_FF, D_MODEL))],
        out_specs=row(D_MODEL),
        out_shape=jax.ShapeDtypeStruct((n, D_MODEL), F32),
        scratch_shapes=[pltpu.VMEM((tile + 2 * BLOCK, KV_DIM), BF16),
                        pltpu.VMEM((KV_DIM, tile + 2 * BLOCK), BF16),
                        pltpu.VMEM((tile, D_MODEL), BF16)],
        compiler_params=pltpu.CompilerParams(dimension_semantics=("arbitrary",),
                                             vmem_limit_bytes=VMEM_LIMIT_BYTES),
        name="mixer",
    )(x2d, gya, q, k, k, k, vt, vt, vt, gb, tbl, sink, wo, n2g, w1, w2)


def kernel(x_prompt, x_sample, rel_bias, norm1_g, w_in, sgu_norm_g, w_spatial, b_spatial,
           q_norm_g, k_norm_g, sink_logit, w_o, norm2_g, w_ff1, w_ff2):
    assert norm1_g.shape[0] == 1, "single-layer trunk"
    tbl = _bias_table(rel_bias)
    g1 = norm1_g[0].reshape(1, D_MODEL)
    sgu_g = sgu_norm_g[0].reshape(1, D_MODEL)
    q_g = q_norm_g[0].reshape(1, HEAD_DIM)
    k_g = k_norm_g[0].reshape(1, HEAD_DIM)
    n2g = norm2_g[0].reshape(1, D_MODEL)
    w_in_b = w_in[0].astype(BF16)
    ws = w_spatial[0].astype(BF16)
    bs = jnp.broadcast_to(b_spatial[0][:, :, None], (SGU_GROUPS, CHUNK, CHUNK))
    wo = w_o[0].astype(BF16)
    w1 = w_ff1[0].astype(BF16)
    w2 = w_ff2[0].astype(BF16)
    sink = sink_logit[0]

    def trunk(x):
        b, s, d = x.shape
        x2d = x.reshape(b * s, d)
        acts = _inproj(x2d, g1, w_in_b, w_in, sgu_g, q_g, k_g, ws, bs, tile=1024)
        y = _mixer(x2d, acts, tbl, sink, wo, n2g, w1, w2, seq=s, tile=512)
        return y.reshape(b, s, d)

    return (trunk(x_prompt), trunk(x_sample))
```

```python
import functools
import math

import jax
import jax.numpy as jnp
from jax import lax
from jax.experimental import pallas as pl
from jax.experimental.pallas import tpu as pltpu

D_MODEL = 1024
CHUNK = 128
SGU_GROUPS = 8
N_HEADS = 8
N_KV_HEADS = 2
HEAD_DIM = 128
Q_PER_KV = N_HEADS // N_KV_HEADS
WINDOW = 128
BLOCK = 128
BAND = 3 * BLOCK
N_BUCKETS = 32
MAX_DISTANCE = 128
D_FF = 4 * D_MODEL
EPS = 1e-6
NEG_INF = -1e30
LOG2E = math.log2(math.e)
KV_DIM = N_KV_HEADS * HEAD_DIM
KV_COLS = 2 * KV_DIM
IN_COLS = 2 * D_MODEL + D_MODEL + KV_COLS + 2 * D_MODEL
PROJ_COLS = 256
PROJ_ROWS = 256
GROUP_BLOCKS = (2, 2)

F32 = jnp.float32
BF16 = jnp.bfloat16

VMEM_LIMIT_BYTES = 56 * 1024 * 1024


def _t5_bucket(rel):
    nb = N_BUCKETS // 2
    max_exact = nb // 2
    ret = jnp.where(rel > 0, nb, 0)
    n = jnp.abs(rel)
    nf = jnp.maximum(n, 1).astype(F32)
    large = max_exact + (jnp.log(nf / max_exact) / math.log(MAX_DISTANCE / max_exact)
                         * (nb - max_exact)).astype(jnp.int32)
    large = jnp.minimum(large, nb - 1)
    return ret + jnp.where(n < max_exact, n, large)


def _rms(x, g):
    ms = jnp.mean(x * x, axis=-1, keepdims=True)
    return x * lax.rsqrt(ms + EPS) * g


def _sigmoid_denominator(x):
    return 1.0 + jnp.exp2(x * (-LOG2E))


def _gelu_denominator(x):
    c = -2.0 * LOG2E * math.sqrt(2.0 / math.pi)
    return 1.0 + jnp.exp2(x * (c + (c * 0.044715) * (x * x)))


def _bias_table_body(bucket_ref, relb_ref, out_ref):
    bucket = bucket_ref[...]
    r = lax.broadcasted_iota(jnp.int32, (BAND, BLOCK), 0)
    p = lax.broadcasted_iota(jnp.int32, (BAND, BLOCK), 1)
    in_win = jnp.abs(r - BLOCK - p) <= WINDOW
    for h in range(N_HEADS):
        kh, g = divmod(h, Q_PER_KV)
        lanes = slice(g * BLOCK, (g + 1) * BLOCK)
        acc = jnp.zeros((BAND, BLOCK), F32)
        for b in range(N_BUCKETS):
            acc = jnp.where(bucket == b, relb_ref[b, h], acc)
        base = jnp.where(in_win, acc * LOG2E, NEG_INF)
        out_ref[0, kh, :, lanes] = jnp.where(r >= BLOCK, base, NEG_INF)
        out_ref[1, kh, :, lanes] = base
        out_ref[2, kh, :, lanes] = jnp.where(r < 2 * BLOCK, base, NEG_INF)


def _bias_table(rel_bias):
    r_idx = jnp.arange(BAND, dtype=jnp.int32)[:, None]
    p_idx = jnp.arange(BLOCK, dtype=jnp.int32)[None, :]
    bucket = _t5_bucket(r_idx - BLOCK - p_idx).astype(jnp.int32)
    return pl.pallas_call(
        _bias_table_body,
        out_shape=jax.ShapeDtypeStruct((3, N_KV_HEADS, BAND, Q_PER_KV * BLOCK), F32),
        in_specs=[pl.BlockSpec(memory_space=pltpu.VMEM),
                  pl.BlockSpec(memory_space=pltpu.SMEM)],
        out_specs=pl.BlockSpec(memory_space=pltpu.VMEM),
        name="bias_table",
    )(bucket, rel_bias)


def _inproj_body(x_ref, g1_ref, w_ref, wv_ref, sgug_ref, qg_ref, kg_ref, ws_ref, bs_ref,
                 gya_ref, q_ref, k_ref, vt_ref, gb_ref, h_ref, gv_ref, gu_ref, vn_ref, wvt_ref):
    @pl.when(pl.program_id(0) == 0)
    def _():
        wvt_ref[...] = wv_ref[0].T.astype(BF16)

    h_ref[...] = _rms(x_ref[...], g1_ref[...]).astype(BF16)
    chunks = D_MODEL // PROJ_COLS

    def head_norm(z, g):
        return jnp.concatenate(
            [_rms(z[:, c:c + HEAD_DIM], g) for c in range(0, z.shape[1], HEAD_DIM)], axis=1)

    def project_rows(rows):
        def proj(lo):
            return jnp.dot(h_ref[rows, :], w_ref[:, lo:lo + PROJ_COLS], preferred_element_type=F32)

        ga_lo, gb_lo = 3 * D_MODEL + KV_COLS, 4 * D_MODEL + KV_COLS
        for c in range(chunks):
            cols = slice(c * PROJ_COLS, (c + 1) * PROJ_COLS)
            zu = proj(c * PROJ_COLS)
            gated_u = zu / (_gelu_denominator(zu) * _sigmoid_denominator(proj(ga_lo + c * PROJ_COLS)))
            gu_ref[rows, cols] = gated_u.astype(BF16)

        ssq = jnp.zeros((PROJ_ROWS, 1), F32)
        for c in range(chunks):
            cols = slice(c * PROJ_COLS, (c + 1) * PROJ_COLS)
            zv = proj(D_MODEL + c * PROJ_COLS)
            gv = zv / _gelu_denominator(zv)
            gv_ref[rows, cols] = gv
            ssq = ssq + jnp.sum(gv * gv, axis=-1, keepdims=True)
            gb_ref[rows, cols] = (1.0 / _sigmoid_denominator(proj(gb_lo + c * PROJ_COLS))).astype(BF16)
        vscale = lax.rsqrt(ssq * (1.0 / D_MODEL) + EPS)
        vn_ref[rows, :] = (gv_ref[rows, :] * vscale * sgug_ref[...]).astype(BF16)

        chunk_rows = [slice(rows.start + t, rows.start + t + CHUNK) for t in range(0, PROJ_ROWS, CHUNK)]

        def sgu_group(g):
            gcols = slice(g * CHUNK, (g + 1) * CHUNK)
            v_wide = jnp.concatenate([vn_ref[cr, gcols] for cr in chunk_rows], axis=1)
            mixed = jnp.dot(ws_ref[g], v_wide, preferred_element_type=F32)
            for t, cr in enumerate(chunk_rows):
                ya = mixed[:, t * CHUNK:(t + 1) * CHUNK] + bs_ref[g]
                gya_ref[cr, gcols] = (gu_ref[cr, gcols].astype(F32) * ya).astype(BF16)

        k_ref[rows, :] = head_norm(proj(3 * D_MODEL), kg_ref[...]).astype(BF16)
        qg = qg_ref[...] * (HEAD_DIM ** -0.5 * LOG2E)
        groups_per_chunk = SGU_GROUPS // chunks
        for c in range(chunks):
            cols = slice(c * PROJ_COLS, (c + 1) * PROJ_COLS)
            q_ref[rows, cols] = head_norm(proj(2 * D_MODEL + c * PROJ_COLS), qg).astype(BF16)
            for g in range(c * groups_per_chunk, (c + 1) * groups_per_chunk):
                sgu_group(g)

    for r in range(0, x_ref.shape[0], PROJ_ROWS):
        project_rows(slice(r, r + PROJ_ROWS))
    for r in range(0, x_ref.shape[0], PROJ_ROWS):
        rows = slice(r, r + PROJ_ROWS)
        vt_ref[:, rows] = lax.dot_general(wvt_ref[...], h_ref[rows, :], (((1,), (1,)), ((), ())),
                                          preferred_element_type=F32).astype(BF16)


def _inproj(x2d, g1, w_in, w_in_f32, sgu_g, q_g, k_g, ws, bs, tile):
    n = x2d.shape[0]
    assert n % tile == 0 and tile % PROJ_ROWS == 0
    row = lambda width: pl.BlockSpec((tile, width), lambda i: (i, 0))
    const = lambda shape: pl.BlockSpec(shape, lambda i: (0,) * len(shape), pipeline_mode=pl.Buffered(1))
    out = lambda width: jax.ShapeDtypeStruct((n, width), BF16)
    vt_spec = pl.BlockSpec((KV_DIM, tile), lambda i: (0, i))
    vt_out = jax.ShapeDtypeStruct((KV_DIM, n), BF16)
    v_block = (3 * D_MODEL + KV_DIM) // KV_DIM
    wv_spec = pl.BlockSpec((1, D_MODEL, KV_DIM), lambda i: (0, 0, v_block), pipeline_mode=pl.Buffered(1))
    return pl.pallas_call(
        _inproj_body,
        grid=(n // tile,),
        in_specs=[row(D_MODEL), const((1, D_MODEL)), const((D_MODEL, IN_COLS)), wv_spec,
                  const((1, D_MODEL)), const((1, HEAD_DIM)), const((1, HEAD_DIM)),
                  const((SGU_GROUPS, CHUNK, CHUNK)), const((SGU_GROUPS, CHUNK, CHUNK))],
        out_specs=[row(D_MODEL), row(D_MODEL), row(KV_DIM), vt_spec, row(D_MODEL)],
        out_shape=[out(D_MODEL), out(D_MODEL), out(KV_DIM), vt_out, out(D_MODEL)],
        scratch_shapes=[pltpu.VMEM((tile, D_MODEL), BF16), pltpu.VMEM((tile, D_MODEL), F32),
                        pltpu.VMEM((tile, D_MODEL), BF16), pltpu.VMEM((tile, D_MODEL), BF16),
                        pltpu.VMEM((KV_DIM, D_MODEL), BF16)],
        compiler_params=pltpu.CompilerParams(dimension_semantics=("arbitrary",),
                                             vmem_limit_bytes=VMEM_LIMIT_BYTES),
        name="inproj",
    )(x2d, g1, w_in, w_in_f32, sgu_g, q_g, k_g, ws, bs)


def _mixer_body(tiles_per_seq, nblk,
                x_ref, gya_ref, q_ref, k_ref, kp_ref, kn_ref, vt_ref, vtp_ref, vtn_ref, gb_ref,
                tbl_ref, sink_ref, wo_ref, n2g_ref, w1_ref, w2_ref,
                o_ref, kband_ref, vtband_ref, merged_ref):
    tile = x_ref.shape[0]
    jt = pl.program_id(0) % tiles_per_seq

    kband_ref[0:BLOCK, :] = kp_ref[...]
    kband_ref[BLOCK:BLOCK + tile, :] = k_ref[...]
    kband_ref[BLOCK + tile:, :] = kn_ref[...]
    vtband_ref[:, 0:BLOCK] = vtp_ref[...]
    vtband_ref[:, BLOCK:BLOCK + tile] = vt_ref[...]
    vtband_ref[:, BLOCK + tile:] = vtn_ref[...]

    def block(i, kh):
        rows = slice(i * BLOCK, (i + 1) * BLOCK)
        band = slice(i * BLOCK, i * BLOCK + BAND)
        variant = 1
        if i == 0:
            variant = jnp.where(jt == 0, 0, variant)
        if i == nblk - 1:
            variant = jnp.where(jt == tiles_per_seq - 1, 2, variant)
        feat = slice(kh * HEAD_DIM, (kh + 1) * HEAD_DIM)
        heads = [kh * Q_PER_KV + g for g in range(Q_PER_KV)]
        qcat = jnp.concatenate(
            [q_ref[rows, hd * HEAD_DIM:(hd + 1) * HEAD_DIM] for hd in heads], axis=0)
        st = lax.dot_general(kband_ref[band, feat], qcat, (((1,), (1,)), ((), ())),
                             preferred_element_type=F32)
        yield
        probs, inv_denoms = [], []
        for g, hd in enumerate(heads):
            lanes = slice(g * BLOCK, (g + 1) * BLOCK)
            sg = st[:, lanes] + tbl_ref[variant, kh, :, lanes]
            sink = sink_ref[hd] * LOG2E
            m = jnp.maximum(jnp.max(sg, axis=0, keepdims=True), sink)
            e = jnp.exp2(sg - m)
            inv_denoms.append(1.0 / (jnp.sum(e, axis=0, keepdims=True) + jnp.exp2(sink - m)))
            probs.append(e.astype(BF16))
            yield
        ot = jnp.dot(vtband_ref[feat, band], jnp.concatenate(probs, axis=1),
                     preferred_element_type=F32)
        yield
        for g, hd in enumerate(heads):
            cols = slice(hd * HEAD_DIM, (hd + 1) * HEAD_DIM)
            lanes = slice(g * BLOCK, (g + 1) * BLOCK)
            yb = (ot[:, lanes] * inv_denoms[g]).T.astype(BF16)
            merged_ref[rows, cols] = gya_ref[rows, cols] + gb_ref[rows, cols] * yb
            yield

    def channel_mix(rows):
        x1 = x_ref[rows, :] + jnp.dot(merged_ref[rows, :], wo_ref[...], preferred_element_type=F32)
        h2 = _rms(x1, n2g_ref[...]).astype(BF16)
        yield
        acc = x1
        for c in range(D_FF // D_MODEL):
            sl = slice(c * D_MODEL, (c + 1) * D_MODEL)
            hid = jnp.dot(h2, w1_ref[:, sl], preferred_element_type=F32)
            hid = jnp.square(jnp.maximum(hid, 0.0)).astype(BF16)
            yield
            acc = acc + jnp.dot(hid, w2_ref[sl, :], preferred_element_type=F32)
            yield
        o_ref[rows, :] = acc

    def run(*gens):
        for gen in gens:
            for _ in gen:
                pass

    def interleave(main, filler, per_main):
        for _ in main:
            for _ in range(per_main):
                next(filler, None)
        run(filler)

    assert sum(GROUP_BLOCKS) == nblk
    starts = [sum(GROUP_BLOCKS[:gi]) for gi in range(len(GROUP_BLOCKS))]

    def mix_group(gi):
        live = [block(i, kh) for i in range(starts[gi], starts[gi] + GROUP_BLOCKS[gi])
                for kh in range(N_KV_HEADS)]
        while live:
            for gen in list(live):
                try:
                    next(gen)
                    yield
                except StopIteration:
                    live.remove(gen)

    def ffn_group(gi):
        return channel_mix(slice(starts[gi] * BLOCK, (starts[gi] + GROUP_BLOCKS[gi]) * BLOCK))

    pieces_per_block = N_KV_HEADS * (2 + 2 * Q_PER_KV)
    ffn_pieces = 1 + 2 * (D_FF // D_MODEL)
    run(mix_group(0))
    for gi in range(1, len(GROUP_BLOCKS)):
        interleave(ffn_group(gi - 1), mix_group(gi), -(-GROUP_BLOCKS[gi] * pieces_per_block // ffn_pieces))
    run(ffn_group(len(GROUP_BLOCKS) - 1))


def _mixer(x2d, acts, tbl, sink, wo, n2g, w1, w2, seq, tile):
    gya, q, k, vt, gb = acts
    n = x2d.shape[0]
    assert seq % tile == 0 and tile == sum(GROUP_BLOCKS) * BLOCK and seq >= 2 * BLOCK
    nblk = tile // BLOCK
    tiles_per_seq = seq // tile
    last_block = n // BLOCK - 1
    prev_block = lambda j: jnp.maximum(j * nblk - 1, 0)
    next_block = lambda j: jnp.minimum((j + 1) * nblk, last_block)
    row = lambda width: pl.BlockSpec((tile, width), lambda j: (j, 0))
    const = lambda shape: pl.BlockSpec(shape, lambda j: (0,) * len(shape), pipeline_mode=pl.Buffered(1))
    return pl.pallas_call(
        functools.partial(_mixer_body, tiles_per_seq, nblk),
        grid=(n // tile,),
        in_specs=[row(D_MODEL), row(D_MODEL), row(D_MODEL),
                  row(KV_DIM),
                  pl.BlockSpec((BLOCK, KV_DIM), lambda j: (prev_block(j), 0)),
                  pl.BlockSpec((BLOCK, KV_DIM), lambda j: (next_block(j), 0)),
                  pl.BlockSpec((KV_DIM, tile), lambda j: (0, j)),
                  pl.BlockSpec((KV_DIM, BLOCK), lambda j: (0, prev_block(j))),
                  pl.BlockSpec((KV_DIM, BLOCK), lambda j: (0, next_block(j))),
                  row(D_MODEL),
                  const((3, N_KV_HEADS, BAND, Q_PER_KV * BLOCK)),
                  pl.BlockSpec(memory_space=pltpu.SMEM),
                  const((D_MODEL, D_MODEL)), const((1, D_MODEL)),
                  const((D_MODEL, D_FF)), const((D_FF, D_MODEL))],
        out_specs=row(D_MODEL),
        out_shape=jax.ShapeDtypeStruct((n, D_MODEL), F32),
        scratch_shapes=[pltpu.VMEM((tile + 2 * BLOCK, KV_DIM), BF16),
                        pltpu.VMEM((KV_DIM, tile + 2 * BLOCK), BF16),
                        pltpu.VMEM((tile, D_MODEL), BF16)],
        compiler_params=pltpu.CompilerParams(dimension_semantics=("arbitrary",),
                                             vmem_limit_bytes=VMEM_LIMIT_BYTES),
        name="mixer",
    )(x2d, gya, q, k, k, k, vt, vt, vt, gb, tbl, sink, wo, n2g, w1, w2)


def kernel(x_prompt, x_sample, rel_bias, norm1_g, w_in, sgu_norm_g, w_spatial, b_spatial,
           q_norm_g, k_norm_g, sink_logit, w_o, norm2_g, w_ff1, w_ff2):
    assert norm1_g.shape[0] == 1, "single-layer trunk"
    tbl = _bias_table(rel_bias)
    g1 = norm1_g[0].reshape(1, D_MODEL)
    sgu_g = sgu_norm_g[0].reshape(1, D_MODEL)
    q_g = q_norm_g[0].reshape(1, HEAD_DIM)
    k_g = k_norm_g[0].reshape(1, HEAD_DIM)
    n2g = norm2_g[0].reshape(1, D_MODEL)
    w_in_b = w_in[0].astype(BF16)
    ws = w_spatial[0].astype(BF16)
    bs = jnp.broadcast_to(b_spatial[0][:, :, None], (SGU_GROUPS, CHUNK, CHUNK))
    wo = w_o[0].astype(BF16)
    w1 = w_ff1[0].astype(BF16)
    w2 = w_ff2[0].astype(BF16)
    sink = sink_logit[0]

    def trunk(x):
        b, s, d = x.shape
        x2d = x.reshape(b * s, d)
        acts = _inproj(x2d, g1, w_in_b, w_in, sgu_g, q_g, k_g, ws, bs, tile=1024)
        y = _mixer(x2d, acts, tbl, sink, wo, n2g, w1, w2, seq=s, tile=512)
        return y.reshape(b, s, d)

    return (trunk(x_prompt), trunk(x_sample))
```

```python
import functools
import math

import jax
import jax.numpy as jnp
from jax import lax
from jax.experimental import pallas as pl
from jax.experimental.pallas import tpu as pltpu

D_MODEL = 1024
CHUNK = 128
SGU_GROUPS = 8
N_HEADS = 8
N_KV_HEADS = 2
HEAD_DIM = 128
Q_PER_KV = N_HEADS // N_KV_HEADS
WINDOW = 128
BLOCK = 128
BAND = 3 * BLOCK
N_BUCKETS = 32
MAX_DISTANCE = 128
D_FF = 4 * D_MODEL
EPS = 1e-6
NEG_INF = -1e30
LOG2E = math.log2(math.e)
KV_DIM = N_KV_HEADS * HEAD_DIM
KV_COLS = 2 * KV_DIM
IN_COLS = 2 * D_MODEL + D_MODEL + KV_COLS + 2 * D_MODEL
PROJ_COLS = 256
PROJ_ROWS = 256
GROUP_BLOCKS = (2, 2)

F32 = jnp.float32
BF16 = jnp.bfloat16
F32_BYTES, BF16_BYTES = 4, 2

TEMP_SLABS = 8


def _vmem_limit(pipelined_bytes, resident_bytes, scratch_bytes):
    temporaries = TEMP_SLABS * PROJ_ROWS * D_MODEL * F32_BYTES
    return 2 * pipelined_bytes + resident_bytes + scratch_bytes + temporaries


def _t5_bucket(rel):
    nb = N_BUCKETS // 2
    max_exact = nb // 2
    ret = jnp.where(rel > 0, nb, 0)
    n = jnp.abs(rel)
    nf = jnp.maximum(n, 1).astype(F32)
    large = max_exact + (jnp.log(nf / max_exact) / math.log(MAX_DISTANCE / max_exact)
                         * (nb - max_exact)).astype(jnp.int32)
    large = jnp.minimum(large, nb - 1)
    return ret + jnp.where(n < max_exact, n, large)


def _rms(x, g):
    ms = jnp.mean(x * x, axis=-1, keepdims=True)
    return x * lax.rsqrt(ms + EPS) * g


def _sigmoid_denominator(x):
    return 1.0 + jnp.exp2(x * (-LOG2E))


def _gelu_denominator(x):
    c = -2.0 * LOG2E * math.sqrt(2.0 / math.pi)
    return 1.0 + jnp.exp2(x * (c + (c * 0.044715) * (x * x)))


def _bias_table_body(bucket_ref, relb_ref, out_ref):
    bucket = bucket_ref[...]
    r = lax.broadcasted_iota(jnp.int32, (BAND, BLOCK), 0)
    p = lax.broadcasted_iota(jnp.int32, (BAND, BLOCK), 1)
    in_win = jnp.abs(r - BLOCK - p) <= WINDOW
    for h in range(N_HEADS):
        kh, g = divmod(h, Q_PER_KV)
        lanes = slice(g * BLOCK, (g + 1) * BLOCK)
        acc = jnp.zeros((BAND, BLOCK), F32)
        for b in range(N_BUCKETS):
            acc = jnp.where(bucket == b, relb_ref[b, h], acc)
        base = jnp.where(in_win, acc * LOG2E, NEG_INF)
        out_ref[0, kh, :, lanes] = jnp.where(r >= BLOCK, base, NEG_INF)
        out_ref[1, kh, :, lanes] = base
        out_ref[2, kh, :, lanes] = jnp.where(r < 2 * BLOCK, base, NEG_INF)


def _bias_table(rel_bias):
    r_idx = jnp.arange(BAND, dtype=jnp.int32)[:, None]
    p_idx = jnp.arange(BLOCK, dtype=jnp.int32)[None, :]
    bucket = _t5_bucket(r_idx - BLOCK - p_idx).astype(jnp.int32)
    return pl.pallas_call(
        _bias_table_body,
        out_shape=jax.ShapeDtypeStruct((3, N_KV_HEADS, BAND, Q_PER_KV * BLOCK), F32),
        in_specs=[pl.BlockSpec(memory_space=pltpu.VMEM),
                  pl.BlockSpec(memory_space=pltpu.SMEM)],
        out_specs=pl.BlockSpec(memory_space=pltpu.VMEM),
        name="bias_table",
    )(bucket, rel_bias)


def _inproj_body(x_ref, g1_ref, w_ref, wv_ref, sgug_ref, qg_ref, kg_ref, ws_ref, bs_ref,
                 gya_ref, q_ref, k_ref, vt_ref, gb_ref, h_ref, gv_ref, gu_ref, vn_ref, wvt_ref):
    @pl.when(pl.program_id(0) == 0)
    def _():
        wvt_ref[...] = wv_ref[0].T.astype(BF16)

    h_ref[...] = _rms(x_ref[...], g1_ref[...]).astype(BF16)
    chunks = D_MODEL // PROJ_COLS

    def head_norm(z, g):
        return jnp.concatenate(
            [_rms(z[:, c:c + HEAD_DIM], g) for c in range(0, z.shape[1], HEAD_DIM)], axis=1)

    def project_rows(rows):
        def proj(lo):
            return jnp.dot(h_ref[rows, :], w_ref[:, lo:lo + PROJ_COLS], preferred_element_type=F32)

        ga_lo, gb_lo = 3 * D_MODEL + KV_COLS, 4 * D_MODEL + KV_COLS
        for c in range(chunks):
            cols = slice(c * PROJ_COLS, (c + 1) * PROJ_COLS)
            zu = proj(c * PROJ_COLS)
            gated_u = zu / (_gelu_denominator(zu) * _sigmoid_denominator(proj(ga_lo + c * PROJ_COLS)))
            gu_ref[rows, cols] = gated_u.astype(BF16)

        ssq = jnp.zeros((PROJ_ROWS, 1), F32)
        for c in range(chunks):
            cols = slice(c * PROJ_COLS, (c + 1) * PROJ_COLS)
            zv = proj(D_MODEL + c * PROJ_COLS)
            gv = zv / _gelu_denominator(zv)
            gv_ref[rows, cols] = gv
            ssq = ssq + jnp.sum(gv * gv, axis=-1, keepdims=True)
            gb_ref[rows, cols] = (1.0 / _sigmoid_denominator(proj(gb_lo + c * PROJ_COLS))).astype(BF16)
        vscale = lax.rsqrt(ssq * (1.0 / D_MODEL) + EPS)
        vn_ref[rows, :] = (gv_ref[rows, :] * vscale * sgug_ref[...]).astype(BF16)

        chunk_rows = [slice(rows.start + t, rows.start + t + CHUNK) for t in range(0, PROJ_ROWS, CHUNK)]

        def sgu_group(g):
            gcols = slice(g * CHUNK, (g + 1) * CHUNK)
            v_wide = jnp.concatenate([vn_ref[cr, gcols] for cr in chunk_rows], axis=1)
            mixed = jnp.dot(ws_ref[g], v_wide, preferred_element_type=F32)
            for t, cr in enumerate(chunk_rows):
                ya = mixed[:, t * CHUNK:(t + 1) * CHUNK] + bs_ref[g]
                gya_ref[cr, gcols] = (gu_ref[cr, gcols].astype(F32) * ya).astype(BF16)

        k_ref[rows, :] = head_norm(proj(3 * D_MODEL), kg_ref[...]).astype(BF16)
        qg = qg_ref[...] * (HEAD_DIM ** -0.5 * LOG2E)
        groups_per_chunk = SGU_GROUPS // chunks
        for c in range(chunks):
            cols = slice(c * PROJ_COLS, (c + 1) * PROJ_COLS)
            q_ref[rows, cols] = head_norm(proj(2 * D_MODEL + c * PROJ_COLS), qg).astype(BF16)
            for g in range(c * groups_per_chunk, (c + 1) * groups_per_chunk):
                sgu_group(g)

    for r in range(0, x_ref.shape[0], PROJ_ROWS):
        project_rows(slice(r, r + PROJ_ROWS))
    for r in range(0, x_ref.shape[0], PROJ_ROWS):
        rows = slice(r, r + PROJ_ROWS)
        vt_ref[:, rows] = lax.dot_general(wvt_ref[...], h_ref[rows, :], (((1,), (1,)), ((), ())),
                                          preferred_element_type=F32).astype(BF16)


def _inproj(x2d, g1, w_in, w_in_f32, sgu_g, q_g, k_g, ws, bs, tile):
    n = x2d.shape[0]
    assert n % tile == 0 and tile % PROJ_ROWS == 0
    row = lambda width: pl.BlockSpec((tile, width), lambda i: (i, 0))
    const = lambda shape: pl.BlockSpec(shape, lambda i: (0,) * len(shape), pipeline_mode=pl.Buffered(1))
    out = lambda width: jax.ShapeDtypeStruct((n, width), BF16)
    vt_spec = pl.BlockSpec((KV_DIM, tile), lambda i: (0, i))
    vt_out = jax.ShapeDtypeStruct((KV_DIM, n), BF16)
    v_block = (3 * D_MODEL + KV_DIM) // KV_DIM
    wv_spec = pl.BlockSpec((1, D_MODEL, KV_DIM), lambda i: (0, 0, v_block), pipeline_mode=pl.Buffered(1))
    out_cols = 3 * D_MODEL + 2 * KV_DIM
    vmem_limit = _vmem_limit(
        pipelined_bytes=tile * (D_MODEL * F32_BYTES + out_cols * BF16_BYTES),
        resident_bytes=(D_MODEL * IN_COLS * BF16_BYTES + D_MODEL * KV_DIM * F32_BYTES
                        + SGU_GROUPS * CHUNK * CHUNK * (BF16_BYTES + F32_BYTES)),
        scratch_bytes=tile * D_MODEL * (3 * BF16_BYTES + F32_BYTES) + KV_DIM * D_MODEL * BF16_BYTES)
    return pl.pallas_call(
        _inproj_body,
        grid=(n // tile,),
        in_specs=[row(D_MODEL), const((1, D_MODEL)), const((D_MODEL, IN_COLS)), wv_spec,
                  const((1, D_MODEL)), const((1, HEAD_DIM)), const((1, HEAD_DIM)),
                  const((SGU_GROUPS, CHUNK, CHUNK)), const((SGU_GROUPS, CHUNK, CHUNK))],
        out_specs=[row(D_MODEL), row(D_MODEL), row(KV_DIM), vt_spec, row(D_MODEL)],
        out_shape=[out(D_MODEL), out(D_MODEL), out(KV_DIM), vt_out, out(D_MODEL)],
        scratch_shapes=[pltpu.VMEM((tile, D_MODEL), BF16), pltpu.VMEM((tile, D_MODEL), F32),
                        pltpu.VMEM((tile, D_MODEL), BF16), pltpu.VMEM((tile, D_MODEL), BF16),
                        pltpu.VMEM((KV_DIM, D_MODEL), BF16)],
        compiler_params=pltpu.CompilerParams(dimension_semantics=("arbitrary",),
                                             vmem_limit_bytes=vmem_limit),
        name="inproj",
    )(x2d, g1, w_in, w_in_f32, sgu_g, q_g, k_g, ws, bs)


def _mixer_body(tiles_per_seq, nblk,
                x_ref, gya_ref, q_ref, k_ref, kp_ref, kn_ref, vt_ref, vtp_ref, vtn_ref, gb_ref,
                tbl_ref, sink_ref, wo_ref, n2g_ref, w1_ref, w2_ref,
                o_ref, kband_ref, vtband_ref, merged_ref):
    tile = x_ref.shape[0]
    jt = pl.program_id(0) % tiles_per_seq

    kband_ref[0:BLOCK, :] = kp_ref[...]
    kband_ref[BLOCK:BLOCK + tile, :] = k_ref[...]
    kband_ref[BLOCK + tile:, :] = kn_ref[...]
    vtband_ref[:, 0:BLOCK] = vtp_ref[...]
    vtband_ref[:, BLOCK:BLOCK + tile] = vt_ref[...]
    vtband_ref[:, BLOCK + tile:] = vtn_ref[...]

    def block(i, kh):
        rows = slice(i * BLOCK, (i + 1) * BLOCK)
        band = slice(i * BLOCK, i * BLOCK + BAND)
        variant = 1
        if i == 0:
            variant = jnp.where(jt == 0, 0, variant)
        if i == nblk - 1:
            variant = jnp.where(jt == tiles_per_seq - 1, 2, variant)
        feat = slice(kh * HEAD_DIM, (kh + 1) * HEAD_DIM)
        heads = [kh * Q_PER_KV + g for g in range(Q_PER_KV)]
        qcat = jnp.concatenate(
            [q_ref[rows, hd * HEAD_DIM:(hd + 1) * HEAD_DIM] for hd in heads], axis=0)
        st = lax.dot_general(kband_ref[band, feat], qcat, (((1,), (1,)), ((), ())),
                             preferred_element_type=F32)
        yield
        probs, inv_denoms = [], []
        for g, hd in enumerate(heads):
            lanes = slice(g * BLOCK, (g + 1) * BLOCK)
            sg = st[:, lanes] + tbl_ref[variant, kh, :, lanes]
            sink = sink_ref[hd] * LOG2E
            m = jnp.maximum(jnp.max(sg, axis=0, keepdims=True), sink)
            e = jnp.exp2(sg - m)
            inv_denoms.append(1.0 / (jnp.sum(e, axis=0, keepdims=True) + jnp.exp2(sink - m)))
            probs.append(e.astype(BF16))
            yield
        ot = jnp.dot(vtband_ref[feat, band], jnp.concatenate(probs, axis=1),
                     preferred_element_type=F32)
        yield
        for g, hd in enumerate(heads):
            cols = slice(hd * HEAD_DIM, (hd + 1) * HEAD_DIM)
            lanes = slice(g * BLOCK, (g + 1) * BLOCK)
            yb = (ot[:, lanes] * inv_denoms[g]).T.astype(BF16)
            merged_ref[rows, cols] = gya_ref[rows, cols] + gb_ref[rows, cols] * yb
            yield

    def channel_mix(rows):
        x1 = x_ref[rows, :] + jnp.dot(merged_ref[rows, :], wo_ref[...], preferred_element_type=F32)
        h2 = _rms(x1, n2g_ref[...]).astype(BF16)
        yield
        acc = x1
        for c in range(D_FF // D_MODEL):
            sl = slice(c * D_MODEL, (c + 1) * D_MODEL)
            hid = jnp.dot(h2, w1_ref[:, sl], preferred_element_type=F32)
            hid = jnp.square(jnp.maximum(hid, 0.0)).astype(BF16)
            yield
            acc = acc + jnp.dot(hid, w2_ref[sl, :], preferred_element_type=F32)
            yield
        o_ref[rows, :] = acc

    def run(*gens):
        for gen in gens:
            for _ in gen:
                pass

    def interleave(main, filler, per_main):
        for _ in main:
            for _ in range(per_main):
                next(filler, None)
        run(filler)

    assert sum(GROUP_BLOCKS) == nblk
    starts = [sum(GROUP_BLOCKS[:gi]) for gi in range(len(GROUP_BLOCKS))]

    def mix_group(gi):
        live = [block(i, kh) for i in range(starts[gi], starts[gi] + GROUP_BLOCKS[gi])
                for kh in range(N_KV_HEADS)]
        while live:
            for gen in list(live):
                try:
                    next(gen)
                    yield
                except StopIteration:
                    live.remove(gen)

    def ffn_group(gi):
        return channel_mix(slice(starts[gi] * BLOCK, (starts[gi] + GROUP_BLOCKS[gi]) * BLOCK))

    pieces_per_block = N_KV_HEADS * (2 + 2 * Q_PER_KV)
    ffn_pieces = 1 + 2 * (D_FF // D_MODEL)
    run(mix_group(0))
    for gi in range(1, len(GROUP_BLOCKS)):
        interleave(ffn_group(gi - 1), mix_group(gi), -(-GROUP_BLOCKS[gi] * pieces_per_block // ffn_pieces))
    run(ffn_group(len(GROUP_BLOCKS) - 1))


def _mixer(x2d, acts, tbl, sink, wo, n2g, w1, w2, seq, tile):
    gya, q, k, vt, gb = acts
    n = x2d.shape[0]
    assert seq % tile == 0 and tile == sum(GROUP_BLOCKS) * BLOCK and seq >= 2 * BLOCK
    nblk = tile // BLOCK
    tiles_per_seq = seq // tile
    last_block = n // BLOCK - 1
    prev_block = lambda j: jnp.maximum(j * nblk - 1, 0)
    next_block = lambda j: jnp.minimum((j + 1) * nblk, last_block)
    row = lambda width: pl.BlockSpec((tile, width), lambda j: (j, 0))
    const = lambda shape: pl.BlockSpec(shape, lambda j: (0,) * len(shape), pipeline_mode=pl.Buffered(1))
    band_rows = tile + 2 * BLOCK
    vmem_limit = _vmem_limit(
        pipelined_bytes=(tile * D_MODEL * 2 * F32_BYTES + tile * 3 * D_MODEL * BF16_BYTES
                         + 2 * band_rows * KV_DIM * BF16_BYTES),
        resident_bytes=(3 * N_KV_HEADS * BAND * Q_PER_KV * BLOCK * F32_BYTES
                        + (D_MODEL * D_MODEL + 2 * D_MODEL * D_FF) * BF16_BYTES),
        scratch_bytes=(2 * band_rows * KV_DIM + tile * D_MODEL) * BF16_BYTES)
    return pl.pallas_call(
        functools.partial(_mixer_body, tiles_per_seq, nblk),
        grid=(n // tile,),
        in_specs=[row(D_MODEL), row(D_MODEL), row(D_MODEL),
                  row(KV_DIM),
                  pl.BlockSpec((BLOCK, KV_DIM), lambda j: (prev_block(j), 0)),
                  pl.BlockSpec((BLOCK, KV_DIM), lambda j: (next_block(j), 0)),
                  pl.BlockSpec((KV_DIM, tile), lambda j: (0, j)),
                  pl.BlockSpec((KV_DIM, BLOCK), lambda j: (0, prev_block(j))),
                  pl.BlockSpec((KV_DIM, BLOCK), lambda j: (0, next_block(j))),
                  row(D_MODEL),
                  const((3, N_KV_HEADS, BAND, Q_PER_KV * BLOCK)),
                  pl.BlockSpec(memory_space=pltpu.SMEM),
                  const((D_MODEL, D_MODEL)), const((1, D_MODEL)),
                  const((D_MODEL, D_FF)), const((D_FF, D_MODEL))],
        out_specs=row(D_MODEL),
        out_shape=jax.ShapeDtypeStruct((n, D_MODEL), F32),
        scratch_shapes=[pltpu.VMEM((band_rows, KV_DIM), BF16),
                        pltpu.VMEM((KV_DIM, band_rows), BF16),
                        pltpu.VMEM((tile, D_MODEL), BF16)],
        compiler_params=pltpu.CompilerParams(dimension_semantics=("arbitrary",),
                                             vmem_limit_bytes=vmem_limit),
        name="mixer",
    )(x2d, gya, q, k, k, k, vt, vt, vt, gb, tbl, sink, wo, n2g, w1, w2)


def kernel(x_prompt, x_sample, rel_bias, norm1_g, w_in, sgu_norm_g, w_spatial, b_spatial,
           q_norm_g, k_norm_g, sink_logit, w_o, norm2_g, w_ff1, w_ff2):
    assert norm1_g.shape[0] == 1, "single-layer trunk"
    tbl = _bias_table(rel_bias)
    g1 = norm1_g[0].reshape(1, D_MODEL)
    sgu_g = sgu_norm_g[0].reshape(1, D_MODEL)
    q_g = q_norm_g[0].reshape(1, HEAD_DIM)
    k_g = k_norm_g[0].reshape(1, HEAD_DIM)
    n2g = norm2_g[0].reshape(1, D_MODEL)
    w_in_b = w_in[0].astype(BF16)
    ws = w_spatial[0].astype(BF16)
    bs = jnp.broadcast_to(b_spatial[0][:, :, None], (SGU_GROUPS, CHUNK, CHUNK))
    wo = w_o[0].astype(BF16)
    w1 = w_ff1[0].astype(BF16)
    w2 = w_ff2[0].astype(BF16)
    sink = sink_logit[0]

    def trunk(x):
        b, s, d = x.shape
        x2d = x.reshape(b * s, d)
        acts = _inproj(x2d, g1, w_in_b, w_in, sgu_g, q_g, k_g, ws, bs, tile=1024)
        y = _mixer(x2d, acts, tbl, sink, wo, n2g, w1, w2, seq=s, tile=512)
        return y.reshape(b, s, d)

    return (trunk(x_prompt), trunk(x_sample))
```

```python
import functools
import math

import jax
import jax.numpy as jnp
from jax import lax
from jax.experimental import pallas as pl
from jax.experimental.pallas import tpu as pltpu

D_MODEL = 1024
CHUNK = 128
SGU_GROUPS = 8
N_HEADS = 8
N_KV_HEADS = 2
HEAD_DIM = 128
Q_PER_KV = N_HEADS // N_KV_HEADS
WINDOW = 128
BLOCK = 128
BAND = 3 * BLOCK
N_BUCKETS = 32
MAX_DISTANCE = 128
D_FF = 4 * D_MODEL
EPS = 1e-6
NEG_INF = -1e30
LOG2E = math.log2(math.e)
KV_DIM = N_KV_HEADS * HEAD_DIM
KV_COLS = 2 * KV_DIM
IN_COLS = 2 * D_MODEL + D_MODEL + KV_COLS + 2 * D_MODEL
PROJ_COLS = 256
PROJ_ROWS = 256
GROUP_BLOCKS = (2, 2)

F32 = jnp.float32
BF16 = jnp.bfloat16
F32_BYTES, BF16_BYTES = 4, 2

TEMP_SLABS = 8


def _vmem_limit(pipelined_bytes, resident_bytes, scratch_bytes):
    temporaries = TEMP_SLABS * PROJ_ROWS * D_MODEL * F32_BYTES
    return 2 * pipelined_bytes + resident_bytes + scratch_bytes + temporaries


def _t5_bucket(rel):
    nb = N_BUCKETS // 2
    max_exact = nb // 2
    ret = jnp.where(rel > 0, nb, 0)
    n = jnp.abs(rel)
    nf = jnp.maximum(n, 1).astype(F32)
    large = max_exact + (jnp.log(nf / max_exact) / math.log(MAX_DISTANCE / max_exact)
                         * (nb - max_exact)).astype(jnp.int32)
    large = jnp.minimum(large, nb - 1)
    return ret + jnp.where(n < max_exact, n, large)


def _rms(x, g):
    ms = jnp.mean(x * x, axis=-1, keepdims=True)
    return x * lax.rsqrt(ms + EPS) * g


def _rms_unit(x):
    return x * lax.rsqrt(jnp.mean(x * x, axis=-1, keepdims=True) + EPS)


def _sigmoid_denominator(x):
    return 1.0 + jnp.exp2(x * (-LOG2E))


def _gelu_denominator(x):
    c = -2.0 * LOG2E * math.sqrt(2.0 / math.pi)
    return 1.0 + jnp.exp2(x * (c + (c * 0.044715) * (x * x)))


def _bias_table_body(bucket_ref, relb_ref, out_ref):
    bucket = bucket_ref[...]
    r = lax.broadcasted_iota(jnp.int32, (BAND, BLOCK), 0)
    p = lax.broadcasted_iota(jnp.int32, (BAND, BLOCK), 1)
    in_win = jnp.abs(r - BLOCK - p) <= WINDOW
    for h in range(N_HEADS):
        kh, g = divmod(h, Q_PER_KV)
        lanes = slice(g * BLOCK, (g + 1) * BLOCK)
        acc = jnp.zeros((BAND, BLOCK), F32)
        for b in range(N_BUCKETS):
            acc = jnp.where(bucket == b, relb_ref[b, h], acc)
        base = jnp.where(in_win, acc * LOG2E, NEG_INF)
        out_ref[0, kh, :, lanes] = jnp.where(r >= BLOCK, base, NEG_INF)
        out_ref[1, kh, :, lanes] = base
        out_ref[2, kh, :, lanes] = jnp.where(r < 2 * BLOCK, base, NEG_INF)


def _bias_table(rel_bias):
    r_idx = jnp.arange(BAND, dtype=jnp.int32)[:, None]
    p_idx = jnp.arange(BLOCK, dtype=jnp.int32)[None, :]
    bucket = _t5_bucket(r_idx - BLOCK - p_idx).astype(jnp.int32)
    return pl.pallas_call(
        _bias_table_body,
        out_shape=jax.ShapeDtypeStruct((3, N_KV_HEADS, BAND, Q_PER_KV * BLOCK), F32),
        in_specs=[pl.BlockSpec(memory_space=pltpu.VMEM),
                  pl.BlockSpec(memory_space=pltpu.SMEM)],
        out_specs=pl.BlockSpec(memory_space=pltpu.VMEM),
        name="bias_table",
    )(bucket, rel_bias)


def _inproj_body(x_ref, g1_ref, w_ref, wv_ref, sgug_ref, qg_ref, kg_ref, ws_ref, bs_ref,
                 gya_ref, q_ref, k_ref, vt_ref, gb_ref, h_ref, gv_ref, gu_ref, vn_ref, wvt_ref):
    @pl.when(pl.program_id(0) == 0)
    def _():
        wvt_ref[...] = (wv_ref[0].T * g1_ref[...]).astype(BF16)

    h_ref[...] = _rms_unit(x_ref[...]).astype(BF16)
    chunks = D_MODEL // PROJ_COLS

    def head_norm(z, g):
        return jnp.concatenate(
            [_rms(z[:, c:c + HEAD_DIM], g) for c in range(0, z.shape[1], HEAD_DIM)], axis=1)

    def project_rows(rows):
        def proj(lo):
            return jnp.dot(h_ref[rows, :], w_ref[:, lo:lo + PROJ_COLS], preferred_element_type=F32)

        ga_lo, gb_lo = 3 * D_MODEL + KV_COLS, 4 * D_MODEL + KV_COLS
        for c in range(chunks):
            cols = slice(c * PROJ_COLS, (c + 1) * PROJ_COLS)
            zu = proj(c * PROJ_COLS)
            gated_u = zu / (_gelu_denominator(zu) * _sigmoid_denominator(proj(ga_lo + c * PROJ_COLS)))
            gu_ref[rows, cols] = gated_u.astype(BF16)

        ssq = jnp.zeros((PROJ_ROWS, 1), F32)
        for c in range(chunks):
            cols = slice(c * PROJ_COLS, (c + 1) * PROJ_COLS)
            zv = proj(D_MODEL + c * PROJ_COLS)
            gv = zv / _gelu_denominator(zv)
            gv_ref[rows, cols] = gv
            ssq = ssq + jnp.sum(gv * gv, axis=-1, keepdims=True)
            gb_ref[rows, cols] = (1.0 / _sigmoid_denominator(proj(gb_lo + c * PROJ_COLS))).astype(BF16)
        vscale = lax.rsqrt(ssq * (1.0 / D_MODEL) + EPS)
        vn_ref[rows, :] = (gv_ref[rows, :] * vscale * sgug_ref[...]).astype(BF16)

        chunk_rows = [slice(rows.start + t, rows.start + t + CHUNK) for t in range(0, PROJ_ROWS, CHUNK)]

        def sgu_group(g):
            gcols = slice(g * CHUNK, (g + 1) * CHUNK)
            v_wide = jnp.concatenate([vn_ref[cr, gcols] for cr in chunk_rows], axis=1)
            mixed = jnp.dot(ws_ref[g], v_wide, preferred_element_type=F32)
            for t, cr in enumerate(chunk_rows):
                ya = mixed[:, t * CHUNK:(t + 1) * CHUNK] + bs_ref[g]
                gya_ref[cr, gcols] = (gu_ref[cr, gcols].astype(F32) * ya).astype(BF16)

        k_ref[rows, :] = head_norm(proj(3 * D_MODEL), kg_ref[...]).astype(BF16)
        qg = qg_ref[...] * (HEAD_DIM ** -0.5 * LOG2E)
        groups_per_chunk = SGU_GROUPS // chunks
        for c in range(chunks):
            cols = slice(c * PROJ_COLS, (c + 1) * PROJ_COLS)
            q_ref[rows, cols] = head_norm(proj(2 * D_MODEL + c * PROJ_COLS), qg).astype(BF16)
            for g in range(c * groups_per_chunk, (c + 1) * groups_per_chunk):
                sgu_group(g)

    for r in range(0, x_ref.shape[0], PROJ_ROWS):
        project_rows(slice(r, r + PROJ_ROWS))
    for r in range(0, x_ref.shape[0], PROJ_ROWS):
        rows = slice(r, r + PROJ_ROWS)
        vt_ref[:, rows] = lax.dot_general(wvt_ref[...], h_ref[rows, :], (((1,), (1,)), ((), ())),
                                          preferred_element_type=F32).astype(BF16)


def _inproj(x2d, g1, w_in, w_in_f32, sgu_g, q_g, k_g, ws, bs, tile):
    n = x2d.shape[0]
    assert n % tile == 0 and tile % PROJ_ROWS == 0
    row = lambda width: pl.BlockSpec((tile, width), lambda i: (i, 0))
    const = lambda shape: pl.BlockSpec(shape, lambda i: (0,) * len(shape), pipeline_mode=pl.Buffered(1))
    out = lambda width: jax.ShapeDtypeStruct((n, width), BF16)
    vt_spec = pl.BlockSpec((KV_DIM, tile), lambda i: (0, i))
    vt_out = jax.ShapeDtypeStruct((KV_DIM, n), BF16)
    v_block = (3 * D_MODEL + KV_DIM) // KV_DIM
    wv_spec = pl.BlockSpec((1, D_MODEL, KV_DIM), lambda i: (0, 0, v_block), pipeline_mode=pl.Buffered(1))
    out_cols = 3 * D_MODEL + 2 * KV_DIM
    vmem_limit = _vmem_limit(
        pipelined_bytes=tile * (D_MODEL * F32_BYTES + out_cols * BF16_BYTES),
        resident_bytes=(D_MODEL * IN_COLS * BF16_BYTES + D_MODEL * KV_DIM * F32_BYTES
                        + SGU_GROUPS * CHUNK * CHUNK * (BF16_BYTES + F32_BYTES)),
        scratch_bytes=tile * D_MODEL * (3 * BF16_BYTES + F32_BYTES) + KV_DIM * D_MODEL * BF16_BYTES)
    return pl.pallas_call(
        _inproj_body,
        grid=(n // tile,),
        in_specs=[row(D_MODEL), const((1, D_MODEL)), const((D_MODEL, IN_COLS)), wv_spec,
                  const((1, D_MODEL)), const((1, HEAD_DIM)), const((1, HEAD_DIM)),
                  const((SGU_GROUPS, CHUNK, CHUNK)), const((SGU_GROUPS, CHUNK, CHUNK))],
        out_specs=[row(D_MODEL), row(D_MODEL), row(KV_DIM), vt_spec, row(D_MODEL)],
        out_shape=[out(D_MODEL), out(D_MODEL), out(KV_DIM), vt_out, out(D_MODEL)],
        scratch_shapes=[pltpu.VMEM((tile, D_MODEL), BF16), pltpu.VMEM((tile, D_MODEL), F32),
                        pltpu.VMEM((tile, D_MODEL), BF16), pltpu.VMEM((tile, D_MODEL), BF16),
                        pltpu.VMEM((KV_DIM, D_MODEL), BF16)],
        compiler_params=pltpu.CompilerParams(dimension_semantics=("arbitrary",),
                                             vmem_limit_bytes=vmem_limit),
        name="inproj",
    )(x2d, g1, w_in, w_in_f32, sgu_g, q_g, k_g, ws, bs)


def _mixer_body(tiles_per_seq, nblk,
                x_ref, gya_ref, q_ref, k_ref, kp_ref, kn_ref, vt_ref, vtp_ref, vtn_ref, gb_ref,
                tbl_ref, sink_ref, wo_ref, w1_ref, w2_ref,
                o_ref, kband_ref, vtband_ref, merged_ref):
    tile = x_ref.shape[0]
    jt = pl.program_id(0) % tiles_per_seq

    kband_ref[0:BLOCK, :] = kp_ref[...]
    kband_ref[BLOCK:BLOCK + tile, :] = k_ref[...]
    kband_ref[BLOCK + tile:, :] = kn_ref[...]
    vtband_ref[:, 0:BLOCK] = vtp_ref[...]
    vtband_ref[:, BLOCK:BLOCK + tile] = vt_ref[...]
    vtband_ref[:, BLOCK + tile:] = vtn_ref[...]

    def block(i, kh):
        rows = slice(i * BLOCK, (i + 1) * BLOCK)
        band = slice(i * BLOCK, i * BLOCK + BAND)
        variant = 1
        if i == 0:
            variant = jnp.where(jt == 0, 0, variant)
        if i == nblk - 1:
            variant = jnp.where(jt == tiles_per_seq - 1, 2, variant)
        feat = slice(kh * HEAD_DIM, (kh + 1) * HEAD_DIM)
        heads = [kh * Q_PER_KV + g for g in range(Q_PER_KV)]
        qcat = jnp.concatenate(
            [q_ref[rows, hd * HEAD_DIM:(hd + 1) * HEAD_DIM] for hd in heads], axis=0)
        st = lax.dot_general(kband_ref[band, feat], qcat, (((1,), (1,)), ((), ())),
                             preferred_element_type=F32)
        yield
        probs, inv_denoms = [], []
        for g, hd in enumerate(heads):
            lanes = slice(g * BLOCK, (g + 1) * BLOCK)
            sg = st[:, lanes] + tbl_ref[variant, kh, :, lanes]
            sink = sink_ref[hd] * LOG2E
            m = jnp.maximum(jnp.max(sg, axis=0, keepdims=True), sink)
            e = jnp.exp2(sg - m)
            inv_denoms.append(1.0 / (jnp.sum(e, axis=0, keepdims=True) + jnp.exp2(sink - m)))
            probs.append(e.astype(BF16))
            yield
        ot = jnp.dot(vtband_ref[feat, band], jnp.concatenate(probs, axis=1),
                     preferred_element_type=F32)
        yield
        for g, hd in enumerate(heads):
            cols = slice(hd * HEAD_DIM, (hd + 1) * HEAD_DIM)
            lanes = slice(g * BLOCK, (g + 1) * BLOCK)
            yb = (ot[:, lanes] * inv_denoms[g]).T.astype(BF16)
            merged_ref[rows, cols] = gya_ref[rows, cols] + gb_ref[rows, cols] * yb
            yield

    def channel_mix(rows):
        x1 = x_ref[rows, :] + jnp.dot(merged_ref[rows, :], wo_ref[...], preferred_element_type=F32)
        h2 = _rms_unit(x1).astype(BF16)
        yield
        acc = x1
        for c in range(D_FF // D_MODEL):
            sl = slice(c * D_MODEL, (c + 1) * D_MODEL)
            hid = jnp.dot(h2, w1_ref[:, sl], preferred_element_type=F32)
            hid = jnp.square(jnp.maximum(hid, 0.0)).astype(BF16)
            yield
            acc = acc + jnp.dot(hid, w2_ref[sl, :], preferred_element_type=F32)
            yield
        o_ref[rows, :] = acc

    def run(*gens):
        for gen in gens:
            for _ in gen:
                pass

    def interleave(main, filler, per_main):
        for _ in main:
            for _ in range(per_main):
                next(filler, None)
        run(filler)

    assert sum(GROUP_BLOCKS) == nblk
    starts = [sum(GROUP_BLOCKS[:gi]) for gi in range(len(GROUP_BLOCKS))]

    def mix_group(gi):
        live = [block(i, kh) for i in range(starts[gi], starts[gi] + GROUP_BLOCKS[gi])
                for kh in range(N_KV_HEADS)]
        while live:
            for gen in list(live):
                try:
                    next(gen)
                    yield
                except StopIteration:
                    live.remove(gen)

    def ffn_group(gi):
        return channel_mix(slice(starts[gi] * BLOCK, (starts[gi] + GROUP_BLOCKS[gi]) * BLOCK))

    pieces_per_block = N_KV_HEADS * (2 + 2 * Q_PER_KV)
    ffn_pieces = 1 + 2 * (D_FF // D_MODEL)
    run(mix_group(0))
    for gi in range(1, len(GROUP_BLOCKS)):
        interleave(ffn_group(gi - 1), mix_group(gi), -(-GROUP_BLOCKS[gi] * pieces_per_block // ffn_pieces))
    run(ffn_group(len(GROUP_BLOCKS) - 1))


def _mixer(x2d, acts, tbl, sink, wo, w1, w2, seq, tile):
    gya, q, k, vt, gb = acts
    n = x2d.shape[0]
    assert seq % tile == 0 and tile == sum(GROUP_BLOCKS) * BLOCK and seq >= 2 * BLOCK
    nblk = tile // BLOCK
    tiles_per_seq = seq // tile
    last_block = n // BLOCK - 1
    prev_block = lambda j: jnp.maximum(j * nblk - 1, 0)
    next_block = lambda j: jnp.minimum((j + 1) * nblk, last_block)
    row = lambda width: pl.BlockSpec((tile, width), lambda j: (j, 0))
    const = lambda shape: pl.BlockSpec(shape, lambda j: (0,) * len(shape), pipeline_mode=pl.Buffered(1))
    band_rows = tile + 2 * BLOCK
    vmem_limit = _vmem_limit(
        pipelined_bytes=(tile * D_MODEL * 2 * F32_BYTES + tile * 3 * D_MODEL * BF16_BYTES
                         + 2 * band_rows * KV_DIM * BF16_BYTES),
        resident_bytes=(3 * N_KV_HEADS * BAND * Q_PER_KV * BLOCK * F32_BYTES
                        + (D_MODEL * D_MODEL + 2 * D_MODEL * D_FF) * BF16_BYTES),
        scratch_bytes=(2 * band_rows * KV_DIM + tile * D_MODEL) * BF16_BYTES)
    return pl.pallas_call(
        functools.partial(_mixer_body, tiles_per_seq, nblk),
        grid=(n // tile,),
        in_specs=[row(D_MODEL), row(D_MODEL), row(D_MODEL),
                  row(KV_DIM),
                  pl.BlockSpec((BLOCK, KV_DIM), lambda j: (prev_block(j), 0)),
                  pl.BlockSpec((BLOCK, KV_DIM), lambda j: (next_block(j), 0)),
                  pl.BlockSpec((KV_DIM, tile), lambda j: (0, j)),
                  pl.BlockSpec((KV_DIM, BLOCK), lambda j: (0, prev_block(j))),
                  pl.BlockSpec((KV_DIM, BLOCK), lambda j: (0, next_block(j))),
                  row(D_MODEL),
                  const((3, N_KV_HEADS, BAND, Q_PER_KV * BLOCK)),
                  pl.BlockSpec(memory_space=pltpu.SMEM),
                  const((D_MODEL, D_MODEL)),
                  const((D_MODEL, D_FF)), const((D_FF, D_MODEL))],
        out_specs=row(D_MODEL),
        out_shape=jax.ShapeDtypeStruct((n, D_MODEL), F32),
        scratch_shapes=[pltpu.VMEM((band_rows, KV_DIM), BF16),
                        pltpu.VMEM((KV_DIM, band_rows), BF16),
                        pltpu.VMEM((tile, D_MODEL), BF16)],
        compiler_params=pltpu.CompilerParams(dimension_semantics=("arbitrary",),
                                             vmem_limit_bytes=vmem_limit),
        name="mixer",
    )(x2d, gya, q, k, k, k, vt, vt, vt, gb, tbl, sink, wo, w1, w2)


def kernel(x_prompt, x_sample, rel_bias, norm1_g, w_in, sgu_norm_g, w_spatial, b_spatial,
           q_norm_g, k_norm_g, sink_logit, w_o, norm2_g, w_ff1, w_ff2):
    assert norm1_g.shape[0] == 1, "single-layer trunk"
    tbl = _bias_table(rel_bias)
    g1 = norm1_g[0].reshape(1, D_MODEL)
    sgu_g = sgu_norm_g[0].reshape(1, D_MODEL)
    q_g = q_norm_g[0].reshape(1, HEAD_DIM)
    k_g = k_norm_g[0].reshape(1, HEAD_DIM)
    w_in_b = (w_in[0] * norm1_g[0][:, None]).astype(BF16)
    ws = w_spatial[0].astype(BF16)
    bs = jnp.broadcast_to(b_spatial[0][:, :, None], (SGU_GROUPS, CHUNK, CHUNK))
    wo = w_o[0].astype(BF16)
    w1 = (w_ff1[0] * norm2_g[0][:, None]).astype(BF16)
    w2 = w_ff2[0].astype(BF16)
    sink = sink_logit[0]

    def trunk(x):
        b, s, d = x.shape
        x2d = x.reshape(b * s, d)
        acts = _inproj(x2d, g1, w_in_b, w_in, sgu_g, q_g, k_g, ws, bs, tile=1024)
        y = _mixer(x2d, acts, tbl, sink, wo, w1, w2, seq=s, tile=512)
        return y.reshape(b, s, d)

    return (trunk(x_prompt), trunk(x_sample))
```
